```python
import math
import jax
import jax.numpy as jnp
from jax import lax
import numpy as np


D_MODEL = 2048
BATCH = 8
SEQ = 2048
DEPTH = 2

GRID_W = 64
CTX_LEN = 256
EPS = 1e-6
N_SUB = 3
N_MOD = 3 * N_SUB
MACARON_W = 0.5
D_FF = 5632
D_RNN = D_MODEL // 2
RNN_BLOCKS = 16
RNN_BW = D_RNN // RNN_BLOCKS
RNN_CONV = 4
RG_C = 8.0
D_CONV = D_MODEL // 2
CONV_K = 31
N_HEADS = 8
D_QK = 64
D_V = 128
D_Q = N_HEADS * 2 * D_QK
D_ATT = N_HEADS * D_V
ROPE_BASE = 10000.0
ROPE_N = D_QK // 4
Q_BLOCK = 128
N_BRANCH = 3
IN_SPLITS = (D_RNN, D_RNN, 2 * D_CONV, D_Q, D_Q, D_ATT, N_BRANCH * D_MODEL)
D_IN = sum(IN_SPLITS)
F32 = jnp.float32

kernel_name = 'hybrid_rglru_conformer_diffattn_dit'


def rms_norm(x, g):
    xf = x.astype(F32)
    y = xf * lax.rsqrt(jnp.mean(xf * xf, axis=-1, keepdims=True) + EPS)
    return (y * g.astype(F32)).astype(x.dtype)


def layer_norm(x, g, b):
    xf = x.astype(F32)
    mu = jnp.mean(xf, axis=-1, keepdims=True)
    var = jnp.mean(jnp.square(xf - mu), axis=-1, keepdims=True)
    y = (xf - mu) * lax.rsqrt(var + EPS)
    return (y * g.astype(F32) + b.astype(F32)).astype(x.dtype)


def modulate(h, shift, scale):
    return h * (1 + scale) + shift


def swiglu(h, w_gate, w_up, w_down):
    return (jax.nn.silu(h @ w_gate) * (h @ w_up)) @ w_down


def ffn_half_step(s, mods, k, norm_gain, w_gate, w_up, w_down):
    h = modulate(rms_norm(s, norm_gain), mods[:, :, 3 * k], mods[:, :, 3 * k + 1])
    return s + MACARON_W * mods[:, :, 3 * k + 2] * swiglu(h, w_gate, w_up, w_down)


def split_columns(z):
    offs = []
    acc = 0
    for w in IN_SPLITS[:-1]:
        acc += w
        offs.append(acc)
    return jnp.split(z, offs, axis=-1)


def axial_rope_tables(seq_len):
    rows = seq_len // GRID_W
    row = jnp.repeat(jnp.arange(rows, dtype=jnp.int32), GRID_W).astype(F32)
    col = jnp.tile(jnp.arange(GRID_W, dtype=jnp.int32), rows).astype(F32)
    inv = ROPE_BASE ** (-jnp.arange(ROPE_N, dtype=F32) * 2.0 / (2 * ROPE_N))
    ang = jnp.stack([row[:, None] * inv, col[:, None] * inv], axis=1)
    return jnp.cos(ang), jnp.sin(ang)


def axial_rope(x, cos, sin):
    xs = x.astype(F32).reshape(x.shape[:-1] + (2, 2, ROPE_N))
    x1, x2 = xs[..., 0, :], xs[..., 1, :]
    c = cos[None, :, None, None]
    s = sin[None, :, None, None]
    out = jnp.stack([x1 * c - x2 * s, x2 * c + x1 * s], axis=-2)
    return out.reshape(x.shape).astype(x.dtype)


def depthwise_conv(u, w, b):
    out = lax.conv_general_dilated(
        u, w[:, None, :].astype(u.dtype), window_strides=(1,), padding='SAME',
        dimension_numbers=('NWC', 'WIO', 'NWC'), feature_group_count=u.shape[-1])
    return out + b


def linear_scan(a, b):
    def combine(left, right):
        a_l, b_l = left
        a_r, b_r = right
        return a_l * a_r, a_r * b_l + b_r
    return lax.associative_scan(combine, (a, b), axis=1)[1]


def rglru_coeffs(u, w_r, b_r, w_i, b_i, lam):
    nb, t, _ = u.shape
    ub = u.reshape(nb, t, RNN_BLOCKS, RNN_BW)
    r = jax.nn.sigmoid((jnp.einsum('bthi,hij->bthj', ub, w_r).reshape(nb, t, D_RNN) + b_r).astype(F32))
    i = jax.nn.sigmoid((jnp.einsum('bthi,hij->bthj', ub, w_i).reshape(nb, t, D_RNN) + b_i).astype(F32))
    log_a = -RG_C * r * jax.nn.softplus(-lam.astype(F32))
    a = jnp.exp(log_a)
    b = jnp.sqrt(-jnp.expm1(2.0 * log_a)) * i * u.astype(F32)
    return a, b


def rglru_branch(xl, yl, xc, yc, p, need_ctx):
    n_ctx = xc.shape[1]
    ul = depthwise_conv(xl, p['rnn_conv_w'], p['rnn_conv_b'])
    uc = depthwise_conv(xc, p['rnn_conv_w'], p['rnn_conv_b'])
    hs_ctx = []
    hs_lat = []
    for d in range(2):
        if d == 0:
            seq = jnp.concatenate([uc, ul], axis=1)
        else:
            seq = jnp.concatenate([jnp.flip(uc, 1), jnp.flip(ul, 1)], axis=1)
        a, b = rglru_coeffs(seq, p['rg_w_r'][d], p['rg_b_r'][d], p['rg_w_i'][d],
                            p['rg_b_i'][d], p['rg_lam'][d])
        h = linear_scan(a, b)
        hc_d, hl_d = h[:, :n_ctx], h[:, n_ctx:]
        if d == 1:
            hc_d, hl_d = jnp.flip(hc_d, 1), jnp.flip(hl_d, 1)
        hs_ctx.append(hc_d)
        hs_lat.append(hl_d)
    h_lat = hs_lat[0] + hs_lat[1]
    out_l = (jax.nn.gelu(yl) * h_lat.astype(yl.dtype)) @ p['rnn_w_out']
    if not need_ctx:
        return out_l, None
    h_ctx = hs_ctx[0] + hs_ctx[1]
    out_c = (jax.nn.gelu(yc) * h_ctx.astype(yc.dtype)) @ p['rnn_w_out']
    return out_l, out_c


def conformer_conv(g, p):
    u = g[..., :D_CONV] * jax.nn.sigmoid(g[..., D_CONV:])
    u = depthwise_conv(u, p['cv_dw_w'], p['cv_dw_b'])
    u = layer_norm(u, p['cv_ln_g'], p['cv_ln_b'])
    return jax.nn.silu(u) @ p['cv_w_out']


def diff_attention(ql, kl, vl, qc, kc, vc, p, lam_init, cos, sin, need_ctx):
    nb, seq, _ = ql.shape

    def heads_qk(t):
        return t.reshape(t.shape[0], t.shape[1], N_HEADS, 2, D_QK)

    def heads_v(t):
        return t.reshape(t.shape[0], t.shape[1], N_HEADS, D_V)

    ql = axial_rope(heads_qk(ql), cos, sin)
    kl = axial_rope(heads_qk(kl), cos, sin)
    kc = heads_qk(kc)
    vl, vc = heads_v(vl), heads_v(vc)
    lq = p['da_lam'].astype(F32)
    lam = jnp.exp(jnp.sum(lq[0] * lq[1])) - jnp.exp(jnp.sum(lq[2] * lq[3])) + lam_init
    scale = D_QK ** -0.5

    def attend(q, k, v):
        s = jnp.einsum('bqhcd,bkhcd->bhcqk', q.astype(F32), k.astype(F32)) * scale
        pr = jax.nn.softmax(s, axis=-1)
        w = pr[:, :, 0] - lam * pr[:, :, 1]
        return jnp.einsum('bhqk,bkhd->bqhd', w.astype(v.dtype), v)

    k_all = jnp.concatenate([kl, kc], axis=1)
    v_all = jnp.concatenate([vl, vc], axis=1)
    q_blocks = jnp.swapaxes(ql.reshape(nb, seq // Q_BLOCK, Q_BLOCK, N_HEADS, 2, D_QK), 0, 1)
    o_l = lax.map(lambda qb: attend(qb, k_all, v_all), q_blocks)
    o_l = jnp.swapaxes(o_l, 0, 1).reshape(nb, seq, N_HEADS, D_V)

    def head_out(o):
        o = rms_norm(o, p['da_subln_g']) * (1.0 - lam_init)
        return o.reshape(o.shape[0], o.shape[1], D_ATT) @ p['da_w_o']

    out_l = head_out(o_l)
    if not need_ctx:
        return out_l, None
    out_c = head_out(attend(heads_qk(qc), kc, vc))
    return out_l, out_c


def gated_merge(z, b0, b1, b2):
    g = jax.nn.sigmoid(z.astype(F32)).astype(z.dtype)
    g = g.reshape(z.shape[:-1] + (N_BRANCH, D_MODEL))
    return g[..., 0, :] * b0 + g[..., 1, :] * b1 + g[..., 2, :] * b2


def token_mix(hl, hc, p, lam_init, cos, sin, need_ctx):
    xl, yl, gl, ql, kl, vl, zl = split_columns(hl @ p['w_in'])
    xc, yc, gc, qc, kc, vc, zc = split_columns(hc @ p['w_in'])
    rnn_l, rnn_c = rglru_branch(xl, yl, xc, yc, p, need_ctx)
    att_l, att_c = diff_attention(ql, kl, vl, qc, kc, vc, p, lam_init, cos, sin, need_ctx)
    conv_l = conformer_conv(gl, p)
    out_l = gated_merge(zl, rnn_l, conv_l, att_l) @ p['w_out']
    if not need_ctx:
        return out_l, None
    conv_c = conformer_conv(gc, p)
    out_c = gated_merge(zc, rnn_c, conv_c, att_c) @ p['w_out']
    return out_l, out_c


def setup_inputs(seed: int = 0) -> dict:
    key = jax.random.key(seed)
    ks = jax.random.split(key, 29)
    D = D_MODEL

    def nrm(k, shape, s):
        return jax.random.normal(k, shape, F32) * s

    u = jax.random.uniform(ks[17], (DEPTH, 2, D_RNN), F32, 0.9, 0.999)
    base = u ** (1.0 / RG_C)
    rg_lam = jnp.log(base) - jnp.log1p(-base)
    return {
        'x': nrm(ks[0], (BATCH, SEQ, D), 1.0),
        'c': nrm(ks[1], (BATCH, D), 1.0),
        'ctx': nrm(ks[2], (BATCH, CTX_LEN, D), 1.0),
        'c_ctx': nrm(ks[3], (D,), 1.0),
        'ada_w': nrm(ks[4], (DEPTH, D, N_MOD * D), 0.5 * D ** -0.5),
        'ada_b': nrm(ks[5], (DEPTH, N_MOD * D), 0.01),
        'norm_g': 1.0 + nrm(ks[6], (DEPTH, N_SUB, D), 0.05),
        'ffn_w_gate': nrm(ks[7], (DEPTH, 2, D, D_FF), D ** -0.5),
        'ffn_w_up': nrm(ks[8], (DEPTH, 2, D, D_FF), D ** -0.5),
        'ffn_w_down': nrm(ks[9], (DEPTH, 2, D_FF, D), D_FF ** -0.5),
        'w_in': nrm(ks[10], (DEPTH, D, D_IN), D ** -0.5),
        'rnn_conv_w': nrm(ks[11], (DEPTH, RNN_CONV, D_RNN), RNN_CONV ** -0.5),
        'rnn_conv_b': nrm(ks[12], (DEPTH, D_RNN), 0.01),
        'rg_w_r': nrm(ks[13], (DEPTH, 2, RNN_BLOCKS, RNN_BW, RNN_BW), RNN_BW ** -0.5),
        'rg_b_r': nrm(ks[14], (DEPTH, 2, D_RNN), 0.01),
        'rg_w_i': nrm(ks[15], (DEPTH, 2, RNN_BLOCKS, RNN_BW, RNN_BW), RNN_BW ** -0.5),
        'rg_b_i': nrm(ks[16], (DEPTH, 2, D_RNN), 0.01),
        'rg_lam': rg_lam,
        'rnn_w_out': nrm(ks[18], (DEPTH, D_RNN, D), D_RNN ** -0.5),
        'cv_dw_w': nrm(ks[19], (DEPTH, CONV_K, D_CONV), CONV_K ** -0.5),
        'cv_dw_b': nrm(ks[20], (DEPTH, D_CONV), 0.01),
        'cv_ln_g': 1.0 + nrm(ks[21], (DEPTH, D_CONV), 0.05),
        'cv_ln_b': nrm(ks[22], (DEPTH, D_CONV), 0.01),
        'cv_w_out': nrm(ks[23], (DEPTH, D_CONV, D), D_CONV ** -0.5),
        'da_lam': nrm(ks[24], (DEPTH, 4, D_QK), 0.1),
        'da_subln_g': 1.0 + nrm(ks[25], (DEPTH, D_V), 0.05),
        'da_w_o': nrm(ks[26], (DEPTH, D_ATT, D), D_ATT ** -0.5),
        'w_out': nrm(ks[27], (DEPTH, D, D), D ** -0.5),
        'final_g': 1.0 + nrm(ks[28], (D,), 0.05),
    }


def reference(x, c, ctx, c_ctx, ada_w, ada_b, norm_g, ffn_w_gate, ffn_w_up, ffn_w_down,
              w_in, rnn_conv_w, rnn_conv_b, rg_w_r, rg_b_r, rg_w_i, rg_b_i, rg_lam,
              rnn_w_out, cv_dw_w, cv_dw_b, cv_ln_g, cv_ln_b, cv_w_out, da_lam,
              da_subln_g, da_w_o, w_out, final_g):
    cos, sin = axial_rope_tables(x.shape[1])
    for l in range(DEPTH):
        need_ctx = l < DEPTH - 1
        lam_init = 0.8 - 0.6 * math.exp(-0.3 * l)
        mods_l = (jax.nn.silu(c) @ ada_w[l] + ada_b[l]).reshape(c.shape[0], 1, N_MOD, D_MODEL)
        mods_c = (jax.nn.silu(c_ctx) @ ada_w[l] + ada_b[l]).reshape(1, 1, N_MOD, D_MODEL)
        p = {
            'w_in': w_in[l], 'rnn_conv_w': rnn_conv_w[l], 'rnn_conv_b': rnn_conv_b[l],
            'rg_w_r': rg_w_r[l], 'rg_b_r': rg_b_r[l], 'rg_w_i': rg_w_i[l], 'rg_b_i': rg_b_i[l],
            'rg_lam': rg_lam[l], 'rnn_w_out': rnn_w_out[l], 'cv_dw_w': cv_dw_w[l],
            'cv_dw_b': cv_dw_b[l], 'cv_ln_g': cv_ln_g[l], 'cv_ln_b': cv_ln_b[l],
            'cv_w_out': cv_w_out[l], 'da_lam': da_lam[l], 'da_subln_g': da_subln_g[l],
            'da_w_o': da_w_o[l], 'w_out': w_out[l],
        }
        x = ffn_half_step(x, mods_l, 0, norm_g[l, 0], ffn_w_gate[l, 0], ffn_w_up[l, 0], ffn_w_down[l, 0])
        ctx = ffn_half_step(ctx, mods_c, 0, norm_g[l, 0], ffn_w_gate[l, 0], ffn_w_up[l, 0], ffn_w_down[l, 0])
        hl = modulate(rms_norm(x, norm_g[l, 1]), mods_l[:, :, 3], mods_l[:, :, 4])
        hc = modulate(rms_norm(ctx, norm_g[l, 1]), mods_c[:, :, 3], mods_c[:, :, 4])
        mix_l, mix_c = token_mix(hl, hc, p, lam_init, cos, sin, need_ctx)
        x = x + mods_l[:, :, 5] * mix_l
        x = ffn_half_step(x, mods_l, 2, norm_g[l, 2], ffn_w_gate[l, 1], ffn_w_up[l, 1], ffn_w_down[l, 1])
        if need_ctx:
            ctx = ctx + mods_c[:, :, 5] * mix_c
            ctx = ffn_half_step(ctx, mods_c, 2, norm_g[l, 2], ffn_w_gate[l, 1], ffn_w_up[l, 1], ffn_w_down[l, 1])
    return rms_norm(x, final_g)
```

```python
import functools
import math

import jax
import jax.numpy as jnp
from jax import lax
from jax.experimental import pallas as pl
from jax.experimental.pallas import tpu as pltpu

F32 = jnp.float32
BF16 = jnp.bfloat16

EPS = 1e-6
GRID_W = 64
ROPE_BASE = 10000.0
RG_C = 8.0
MACARON_W = 0.5
N_MOD = 9

LANES = 128
SUBLANES = 8
VMEM_LIMIT = 56 * 1024 * 1024


def _tiles(T):
    return dict(
        tm_ffn=min(512, T), tf=512, tm_in=min(512, T), tn_in=1024, tm_merge=min(256, T),
        tt_conv=min(256, T), tq=min(256, T), tc_rnn=256)


def _params(sem):
    return pltpu.CompilerParams(dimension_semantics=sem, vmem_limit_bytes=VMEM_LIMIT)


def _mod_spec(row_fn, k, D):
    return pl.BlockSpec((None, 1, D), lambda i, *_: (row_fn(i) * N_MOD + k, 0, 0))


def _rms_mod(x, ng, sc, sh):
    ms = jnp.mean(x * x, axis=-1, keepdims=True)
    return (x * lax.rsqrt(ms + EPS) * ng) * (1.0 + sc) + sh


def _ada_kernel(c_ref, w_ref, b_ref, o_ref):
    c = c_ref[...]
    a = (c * jax.nn.sigmoid(c)).astype(BF16)
    o_ref[...] = jnp.dot(a, w_ref[...].astype(BF16), preferred_element_type=F32) + b_ref[...]


def _ada(cc, ada_w, ada_b):
    depth, D, N = ada_w.shape
    R = cc.shape[0]
    tn = min(1024, D)
    assert N % tn == 0
    return pl.pallas_call(
        _ada_kernel,
        grid=(depth, N // tn),
        in_specs=[pl.BlockSpec((R, D), lambda l, j: (0, 0)),
                  pl.BlockSpec((None, D, tn), lambda l, j: (l, 0, j)),
                  pl.BlockSpec((None, 1, tn), lambda l, j: (l, 0, j))],
        out_specs=pl.BlockSpec((None, R, tn), lambda l, j: (l, 0, j)),
        out_shape=jax.ShapeDtypeStruct((depth, R, N), F32),
        compiler_params=_params(("arbitrary", "arbitrary")),
        name="ada_mods",
    )(cc, ada_w, ada_b.reshape(depth, 1, N))


def _ffn_kernel(*refs, final):
    if final:
        s_ref, sh_ref, sc_ref, gt_ref, ng_ref, wg_ref, wu_ref, wd_ref, fg_ref, o_ref, h_scr = refs
    else:
        s_ref, sh_ref, sc_ref, gt_ref, ng_ref, wg_ref, wu_ref, wd_ref, o_ref, h_scr = refs
    j = pl.program_id(1)

    @pl.when(j == 0)
    def _():
        h_scr[...] = _rms_mod(s_ref[...], ng_ref[...], sc_ref[...], sh_ref[...]).astype(BF16)

    h = h_scr[...]
    g = jnp.dot(h, wg_ref[...], preferred_element_type=F32)
    u = jnp.dot(h, wu_ref[...], preferred_element_type=F32)
    a = (g * jax.nn.sigmoid(g) * u).astype(BF16)
    part = jnp.dot(a, wd_ref[...], preferred_element_type=F32)

    @pl.when(j == 0)
    def _():
        o_ref[...] = part

    @pl.when(j > 0)
    def _():
        o_ref[...] += part

    @pl.when(j == pl.num_programs(1) - 1)
    def _():
        y = s_ref[...] + (MACARON_W * gt_ref[...]) * o_ref[...]
        if final:
            y = y * lax.rsqrt(jnp.mean(y * y, axis=-1, keepdims=True) + EPS) * fg_ref[...]
        o_ref[...] = y


def _ffn(s, mods, row_fn, k, ng, wg, wu, wd, tm, tf, final_g=None):
    M, D = s.shape
    Fd = wg.shape[1]
    assert M % tm == 0 and Fd % tf == 0
    in_specs = [pl.BlockSpec((tm, D), lambda i, j: (i, 0)),
                _mod_spec(row_fn, 3 * k, D), _mod_spec(row_fn, 3 * k + 1, D), _mod_spec(row_fn, 3 * k + 2, D),
                pl.BlockSpec((1, D), lambda i, j: (0, 0)),
                pl.BlockSpec((D, tf), lambda i, j: (0, j)),
                pl.BlockSpec((D, tf), lambda i, j: (0, j)),
                pl.BlockSpec((tf, D), lambda i, j: (j, 0))]
    args = [s, mods, mods, mods, ng.reshape(1, D), wg, wu, wd]
    if final_g is not None:
        in_specs.append(pl.BlockSpec((1, D), lambda i, j: (0, 0)))
        args.append(final_g.reshape(1, D))
    return pl.pallas_call(
        functools.partial(_ffn_kernel, final=final_g is not None),
        grid=(M // tm, Fd // tf),
        in_specs=in_specs,
        out_specs=pl.BlockSpec((tm, D), lambda i, j: (i, 0)),
        out_shape=jax.ShapeDtypeStruct((M, D), F32),
        scratch_shapes=[pltpu.VMEM((tm, D), BF16)],
        compiler_params=_params(("parallel", "arbitrary")),
        name="ffn_half_step",
    )(*args)


def _in_kernel(x_ref, sh_ref, sc_ref, ng_ref, w_ref, o_ref, h_scr):
    @pl.when(pl.program_id(1) == 0)
    def _():
        h_scr[...] = _rms_mod(x_ref[...], ng_ref[...], sc_ref[...], sh_ref[...]).astype(BF16)

    o_ref[...] = jnp.dot(h_scr[...], w_ref[...], preferred_element_type=F32).astype(BF16)


def _in_proj(s, mods, row_fn, ng, w, tm, tn):
    M, D = s.shape
    N = w.shape[1]
    assert M % tm == 0 and N % tn == 0
    return pl.pallas_call(
        _in_kernel,
        grid=(M // tm, N // tn),
        in_specs=[pl.BlockSpec((tm, D), lambda i, j: (i, 0)),
                  _mod_spec(row_fn, 3, D), _mod_spec(row_fn, 4, D),
                  pl.BlockSpec((1, D), lambda i, j: (0, 0)),
                  pl.BlockSpec((D, tn), lambda i, j: (0, j))],
        out_specs=pl.BlockSpec((tm, tn), lambda i, j: (i, j)),
        out_shape=jax.ShapeDtypeStruct((M, N), BF16),
        scratch_shapes=[pltpu.VMEM((tm, D), BF16)],
        compiler_params=_params(("parallel", "arbitrary")),
        name="in_proj",
    )(s, mods, mods, ng.reshape(1, D), w)


def _neg_expm1(x):
    u = jnp.exp(x)
    one = u == 1.0
    return -jnp.where(one, x, (u - 1.0) * x / jnp.where(one, 1.0, jnp.log(u)))


def _scan_segment(a_scr, b_scr, h_scr, start, n, reverse, accumulate, h0):
    C = a_scr.shape[1]
    row = lax.broadcasted_iota(jnp.int32, (SUBLANES, C), 0)
    nblk = n // SUBLANES

    def body(i, h):
        blk = (nblk - 1 - i) if reverse else i
        base = pl.multiple_of(start + blk * SUBLANES, SUBLANES)
        A = a_scr[pl.ds(base, SUBLANES), :]
        Bv = b_scr[pl.ds(base, SUBLANES), :]
        for s in (1, 2, 4):
            if reverse:
                valid = row < SUBLANES - s
                sh = SUBLANES - s
            else:
                valid = row >= s
                sh = s
            As = jnp.where(valid, pltpu.roll(A, sh, 0), 1.0)
            Bs = jnp.where(valid, pltpu.roll(Bv, sh, 0), 0.0)
            Bv = A * Bs + Bv
            A = A * As
        H = Bv + A * h
        if accumulate:
            h_scr[pl.ds(base, SUBLANES), :] += H
        else:
            h_scr[pl.ds(base, SUBLANES), :] = H
        last = H[0:1, :] if reverse else H[SUBLANES - 1:SUBLANES, :]
        return jnp.broadcast_to(last, (SUBLANES, C))

    return lax.fori_loop(0, nblk, body, h0)


def _rnn_kernel(*refs, T, Tc, need_ctx):
    if need_ctx:
        (xl_ref, xc_ref, yl_ref, yc_ref, cw_ref, cb_ref, wr_ref, br_ref, wi_ref, bi_ref, lam_ref,
         ol_ref, oc_ref, xp_l, xp_c, u_scr, a_scr, b_scr, h_scr) = refs
    else:
        (xl_ref, xc_ref, yl_ref, cw_ref, cb_ref, wr_ref, br_ref, wi_ref, bi_ref, lam_ref,
         ol_ref, xp_l, xp_c, u_scr, a_scr, b_scr, h_scr) = refs
    C = u_scr.shape[1]
    K = cw_ref.shape[0]
    P = SUBLANES

    def conv(x_ref, xp, off, n):
        xp[0:P, :] = jnp.zeros((P, C), F32)
        xp[P + n:2 * P + n, :] = jnp.zeros((P, C), F32)
        xp[P:P + n, :] = x_ref[...].astype(F32)
        acc = jnp.broadcast_to(cb_ref[...], (n, C))
        lo = (K - 1) // 2
        for k in range(K):
            acc = acc + cw_ref[k:k + 1, :] * xp[pl.ds(P - lo + k, n), :]
        u_scr[off:off + n, :] = acc

    conv(xc_ref, xp_c, 0, Tc)
    conv(xl_ref, xp_l, Tc, T)

    zero = jnp.zeros((SUBLANES, C), F32)
    for d in range(2):
        sp = jax.nn.softplus(-lam_ref[d:d + 1, :])
        for off, n in ((0, Tc), (Tc, T)):
            uu = u_scr[off:off + n, :]
            ub = uu.astype(BF16)
            r = jax.nn.sigmoid(jnp.dot(ub, wr_ref[d], preferred_element_type=F32) + br_ref[d:d + 1, :])
            ig = jax.nn.sigmoid(jnp.dot(ub, wi_ref[d], preferred_element_type=F32) + bi_ref[d:d + 1, :])
            log_a = (-RG_C * r) * sp
            a_scr[off:off + n, :] = jnp.exp(log_a)
            b_scr[off:off + n, :] = jnp.sqrt(_neg_expm1(2.0 * log_a)) * ig * uu
        h = _scan_segment(a_scr, b_scr, h_scr, 0, Tc, d == 1, d == 1, zero)
        _scan_segment(a_scr, b_scr, h_scr, Tc, T, d == 1, d == 1, h)

    ol_ref[...] = (jax.nn.gelu(yl_ref[...].astype(F32)) * h_scr[Tc:Tc + T, :]).astype(BF16)
    if need_ctx:
        oc_ref[...] = (jax.nn.gelu(yc_ref[...].astype(F32)) * h_scr[0:Tc, :]).astype(BF16)


def _rnn(zl, zc, col_x, col_y, cw, cb, wr_bd, br, wi_bd, bi, lam, need_ctx, tc):
    B, T, _ = zl.shape
    Tc = zc.shape[1]
    K, C = cw.shape
    assert C % tc == 0 and col_x % tc == 0 and col_y % tc == 0 and T % SUBLANES == 0 and Tc % SUBLANES == 0
    nt = C // tc
    bx, by = col_x // tc, col_y // tc
    in_specs = [pl.BlockSpec((None, T, tc), lambda b, c: (b, 0, bx + c)),
                pl.BlockSpec((None, Tc, tc), lambda b, c: (b, 0, bx + c)),
                pl.BlockSpec((None, T, tc), lambda b, c: (b, 0, by + c))]
    args = [zl, zc, zl]
    if need_ctx:
        in_specs.append(pl.BlockSpec((None, Tc, tc), lambda b, c: (b, 0, by + c)))
        args.append(zc)
    in_specs += [pl.BlockSpec((K, tc), lambda b, c: (0, c)),
                 pl.BlockSpec((1, tc), lambda b, c: (0, c)),
                 pl.BlockSpec((2, None, tc, tc), lambda b, c: (0, c, 0, 0)),
                 pl.BlockSpec((2, tc), lambda b, c: (0, c)),
                 pl.BlockSpec((2, None, tc, tc), lambda b, c: (0, c, 0, 0)),
                 pl.BlockSpec((2, tc), lambda b, c: (0, c)),
                 pl.BlockSpec((2, tc), lambda b, c: (0, c))]
    args += [cw, cb.reshape(1, C), wr_bd, br, wi_bd, bi, lam]
    out_specs = [pl.BlockSpec((None, T, tc), lambda b, c: (b, 0, c))]
    out_shape = [jax.ShapeDtypeStruct((B, T, C), BF16)]
    if need_ctx:
        out_specs.append(pl.BlockSpec((None, Tc, tc), lambda b, c: (b, 0, c)))
        out_shape.append(jax.ShapeDtypeStruct((B, Tc, C), BF16))
    Tt = T + Tc
    outs = pl.pallas_call(
        functools.partial(_rnn_kernel, T=T, Tc=Tc, need_ctx=need_ctx),
        grid=(B, nt),
        in_specs=in_specs,
        out_specs=out_specs,
        out_shape=out_shape,
        scratch_shapes=[pltpu.VMEM((T + 2 * SUBLANES, tc), F32), pltpu.VMEM((Tc + 2 * SUBLANES, tc), F32),
                        pltpu.VMEM((Tt, tc), F32), pltpu.VMEM((Tt, tc), F32),
                        pltpu.VMEM((Tt, tc), F32), pltpu.VMEM((Tt, tc), F32)],
        compiler_params=_params(("parallel", "parallel")),
        name="rglru",
    )(*args)
    return (outs[0], outs[1]) if need_ctx else (outs[0], None)


def _conv_kernel(cur_ref, prev_ref, next_ref, w_ref, b_ref, lg_ref, lb_ref, o_ref, xp, sh_scr, cv_scr, *, halo, rb, lt):
    i = pl.program_id(1)
    tt, C = o_ref.shape
    K = w_ref.shape[0]
    half = (K - 1) // 2

    def glu(ref):
        g = ref[...].astype(F32)
        return g[:, :C] * jax.nn.sigmoid(g[:, C:])

    xp[0:halo, :] = jnp.where(i > 0, glu(prev_ref), 0.0)
    xp[halo:halo + tt, :] = glu(cur_ref)
    xp[halo + tt:2 * halo + tt, :] = jnp.where(i < pl.num_programs(1) - 1, glu(next_ref), 0.0)

    n_sh = tt + 2 * halo - SUBLANES
    for r in range(SUBLANES):
        sh_scr[r] = xp[pl.ds(r, n_sh), :]

    def blk(bi, carry):
        for li in range(C // lt):
            ls = slice(li * lt, (li + 1) * lt)
            acc = jnp.broadcast_to(b_ref[:, ls], (rb, lt))
            for k in range(K):
                q, r = divmod(halo - half + k, SUBLANES)
                base = pl.multiple_of(bi * rb + q * SUBLANES, SUBLANES)
                acc = acc + w_ref[k:k + 1, ls] * sh_scr[r, pl.ds(base, rb), ls]
            cv_scr[pl.ds(pl.multiple_of(bi * rb, SUBLANES), rb), ls] = acc
        return carry

    lax.fori_loop(0, tt // rb, blk, 0)

    u = cv_scr[...]
    mu = jnp.mean(u, axis=-1, keepdims=True)
    uc = u - mu
    var = jnp.mean(uc * uc, axis=-1, keepdims=True)
    y = uc * lax.rsqrt(var + EPS) * lg_ref[...] + lb_ref[...]
    o_ref[...] = (y * jax.nn.sigmoid(y)).astype(BF16)


def _conformer(z, col_g, w, b, lg, lb, tt):
    B, T, _ = z.shape
    K, C = w.shape
    halo = 16
    assert (K - 1) // 2 <= halo and T % tt == 0 and tt % halo == 0 and col_g % (2 * C) == 0
    bg = col_g // (2 * C)
    th = tt // halo
    nh = T // halo
    rb = min(32, tt)
    lt = min(256, C)
    return pl.pallas_call(
        functools.partial(_conv_kernel, halo=halo, rb=rb, lt=lt),
        grid=(B, T // tt),
        in_specs=[pl.BlockSpec((None, tt, 2 * C), lambda b_, i: (b_, i, bg)),
                  pl.BlockSpec((None, halo, 2 * C), lambda b_, i: (b_, jnp.maximum(i * th - 1, 0), bg)),
                  pl.BlockSpec((None, halo, 2 * C), lambda b_, i: (b_, jnp.minimum((i + 1) * th, nh - 1), bg)),
                  pl.BlockSpec((K, C), lambda b_, i: (0, 0)),
                  pl.BlockSpec((1, C), lambda b_, i: (0, 0)),
                  pl.BlockSpec((1, C), lambda b_, i: (0, 0)),
                  pl.BlockSpec((1, C), lambda b_, i: (0, 0))],
        out_specs=pl.BlockSpec((None, tt, C), lambda b_, i: (b_, i, 0)),
        out_shape=jax.ShapeDtypeStruct((B, T, C), BF16),
        scratch_shapes=[pltpu.VMEM((tt + 2 * halo, C), F32),
                        pltpu.VMEM((SUBLANES, tt + 2 * halo - SUBLANES, C), F32),
                        pltpu.VMEM((tt, C), F32)],
        compiler_params=_params(("parallel", "parallel")),
        name="conformer_conv",
    )(z, z, z, w, b.reshape(1, C), lg.reshape(1, C), lb.reshape(1, C))


def _rope(x, cs, sn):
    lane = lax.broadcasted_iota(jnp.int32, x.shape, 1)
    quarter = x.shape[1] // 8
    partner = jnp.where(lane % (2 * quarter) < quarter,
                        pltpu.roll(x, x.shape[1] - quarter, 1), pltpu.roll(x, quarter, 1))
    return x * cs + partner * sn


def _attn_kernel(*refs, n_src, rope, lam_init, scale):
    it = iter(refs)
    q_ref = next(it)
    kv = [(next(it), next(it)) for _ in range(n_src)]
    if rope:
        cq_ref, sq_ref, ck_ref, sk_ref = next(it), next(it), next(it), next(it)
    dl_ref, g_ref, o_ref = next(it), next(it), next(it)
    if rope:
        k_scr = next(it)
    i = pl.program_id(2)
    tq, hd = q_ref.shape

    q = q_ref[...].astype(F32)
    if rope:
        q = _rope(q, cq_ref[...], sq_ref[...])

        @pl.when(i == 0)
        def _():
            k_scr[...] = _rope(kv[0][0][...].astype(F32), ck_ref[...], sk_ref[...]).astype(BF16)

    qb = (q * scale).astype(BF16)
    ks = [k_scr[...] if (rope and s == 0) else kv[s][0][...] for s in range(n_src)]
    vs = [kv[s][1][...] for s in range(n_src)]

    lq = dl_ref[...]
    lam = (jnp.exp(jnp.sum(lq[0:1] * lq[1:2], axis=-1, keepdims=True))
           - jnp.exp(jnp.sum(lq[2:3] * lq[3:4], axis=-1, keepdims=True)) + lam_init)

    lane = lax.broadcasted_iota(jnp.int32, (tq, hd), 1)
    outs = []
    for c in range(2):
        in_c = (lane < hd // 2) if c == 0 else (lane >= hd // 2)
        qc = jnp.where(in_c, qb, jnp.zeros_like(qb))
        ss = [lax.dot_general(qc, k, (((1,), (1,)), ((), ())), preferred_element_type=F32) for k in ks]
        m = functools.reduce(jnp.maximum, [jnp.max(s, axis=-1, keepdims=True) for s in ss])
        es = [jnp.exp(s - m) for s in ss]
        den = functools.reduce(jnp.add, [jnp.sum(e, axis=-1, keepdims=True) for e in es])
        num = functools.reduce(jnp.add, [jnp.dot(e.astype(BF16), v, preferred_element_type=F32)
                                         for e, v in zip(es, vs)])
        outs.append(num / den)
    o = outs[0] - lam * outs[1]
    y = o * lax.rsqrt(jnp.mean(o * o, axis=-1, keepdims=True) + EPS) * g_ref[...] * (1.0 - lam_init)
    o_ref[...] = y.astype(BF16)


def _attention(zq, srcs, col_q, col_k, col_v, tabs, da_lam, subln_g, lam_init, n_heads, tq):
    B, Tq, _ = zq.shape
    hd = subln_g.shape[0]
    assert hd == LANES and col_q % hd == 0 and col_k % hd == 0 and col_v % hd == 0 and Tq % tq == 0
    bq, bk, bv = col_q // hd, col_k // hd, col_v // hd
    rope = tabs is not None
    in_specs = [pl.BlockSpec((None, tq, hd), lambda b, h, i: (b, i, bq + h))]
    args = [zq]
    for zs in srcs:
        Tk = zs.shape[1]
        in_specs += [pl.BlockSpec((None, Tk, hd), lambda b, h, i: (b, 0, bk + h)),
                     pl.BlockSpec((None, Tk, hd), lambda b, h, i: (b, 0, bv + h))]
        args += [zs, zs]
    scratch = []
    if rope:
        cs, sn = tabs
        in_specs += [pl.BlockSpec((tq, hd), lambda b, h, i: (i, 0)), pl.BlockSpec((tq, hd), lambda b, h, i: (i, 0)),
                     pl.BlockSpec((Tq, hd), lambda b, h, i: (0, 0)), pl.BlockSpec((Tq, hd), lambda b, h, i: (0, 0))]
        args += [cs, sn, cs, sn]
        scratch.append(pltpu.VMEM((Tq, hd), BF16))
    in_specs += [pl.BlockSpec(da_lam.shape, lambda b, h, i: (0, 0)), pl.BlockSpec((1, hd), lambda b, h, i: (0, 0))]
    args += [da_lam, subln_g.reshape(1, hd)]
    d_qk = da_lam.shape[1]
    return pl.pallas_call(
        functools.partial(_attn_kernel, n_src=len(srcs), rope=rope, lam_init=lam_init, scale=d_qk ** -0.5),
        grid=(B, n_heads, Tq // tq),
        in_specs=in_specs,
        out_specs=pl.BlockSpec((None, tq, hd), lambda b, h, i: (b, i, h)),
        out_shape=jax.ShapeDtypeStruct((B, Tq, n_heads * hd), BF16),
        scratch_shapes=scratch,
        compiler_params=_params(("parallel", "parallel", "arbitrary")),
        name="diff_attention",
    )(*args)


def _merge_kernel(r_ref, c_ref, a_ref, z0_ref, z1_ref, z2_ref, x_ref, gt_ref, w0_ref, w1_ref, w2_ref, wo_ref, o_ref):
    def gate(z_ref):
        return jax.nn.sigmoid(z_ref[...].astype(F32))

    m = gate(z0_ref) * jnp.dot(r_ref[...], w0_ref[...], preferred_element_type=F32)
    m = m + gate(z1_ref) * jnp.dot(c_ref[...], w1_ref[...], preferred_element_type=F32)
    m = m + gate(z2_ref) * jnp.dot(a_ref[...], w2_ref[...], preferred_element_type=F32)
    out = jnp.dot(m.astype(BF16), wo_ref[...], preferred_element_type=F32)
    o_ref[...] = x_ref[...] + gt_ref[...] * out


def _merge(rnn_g, conv_g, att_g, z, col_z, s, mods, row_fn, w0, w1, w2, wo, tm):
    M, D = s.shape
    assert M % tm == 0 and col_z % D == 0
    bz = col_z // D

    def act_spec(a):
        return pl.BlockSpec((tm, a.shape[1]), lambda i: (i, 0))

    def w_spec(w):
        return pl.BlockSpec(w.shape, lambda i: (0, 0), pipeline_mode=pl.Buffered(1))

    return pl.pallas_call(
        _merge_kernel,
        grid=(M // tm,),
        in_specs=[act_spec(rnn_g), act_spec(conv_g), act_spec(att_g),
                  pl.BlockSpec((tm, D), lambda i: (i, bz)),
                  pl.BlockSpec((tm, D), lambda i: (i, bz + 1)),
                  pl.BlockSpec((tm, D), lambda i: (i, bz + 2)),
                  pl.BlockSpec((tm, D), lambda i: (i, 0)),
                  _mod_spec(row_fn, 5, D),
                  w_spec(w0), w_spec(w1), w_spec(w2), w_spec(wo)],
        out_specs=pl.BlockSpec((tm, D), lambda i: (i, 0)),
        out_shape=jax.ShapeDtypeStruct((M, D), F32),
        compiler_params=_params(("parallel",)),
        name="merge_out_proj",
    )(rnn_g, conv_g, att_g, z, z, z, s, mods, w0, w1, w2, wo)


def _rope_tables(T, d_qk):
    n = d_qk // 4
    rows = T // GRID_W
    row = jnp.repeat(jnp.arange(rows, dtype=jnp.int32), GRID_W).astype(F32)
    col = jnp.tile(jnp.arange(GRID_W, dtype=jnp.int32), rows).astype(F32)
    inv = ROPE_BASE ** (-jnp.arange(n, dtype=F32) * 2.0 / (2 * n))
    ar, ac = row[:, None] * inv, col[:, None] * inv
    cs = jnp.concatenate([jnp.cos(ar), jnp.cos(ar), jnp.cos(ac), jnp.cos(ac)], axis=1)
    sn = jnp.concatenate([-jnp.sin(ar), jnp.sin(ar), -jnp.sin(ac), jnp.sin(ac)], axis=1)
    return jnp.tile(cs, (1, 2)), jnp.tile(sn, (1, 2))


def _block_diag(w, tc):
    two, nb, bw, _ = w.shape
    per = tc // bw
    w5 = w.reshape(two, nb // per, per, bw, bw)
    eye = jnp.eye(per, dtype=w.dtype)
    bd = jnp.einsum('dtpij,pq->dtpiqj', w5, eye)
    return bd.reshape(two, nb // per, tc, tc).astype(BF16)


def kernel(x, c, ctx, c_ctx, ada_w, ada_b, norm_g, ffn_w_gate, ffn_w_up, ffn_w_down, w_in, rnn_conv_w, rnn_conv_b, rg_w_r, rg_b_r, rg_w_i, rg_b_i, rg_lam, rnn_w_out, cv_dw_w, cv_dw_b, cv_ln_g, cv_ln_b, cv_w_out, da_lam, da_subln_g, da_w_o, w_out, final_g):
    B, T, D = x.shape
    Tc = ctx.shape[1]
    depth = ada_w.shape[0]
    d_rnn = rnn_conv_w.shape[2]
    d_conv = cv_dw_w.shape[2]
    d_qk = da_lam.shape[2]
    d_v = da_subln_g.shape[1]
    d_att = da_w_o.shape[1]
    n_heads = d_att // d_v
    d_q = n_heads * 2 * d_qk
    splits = (d_rnn, d_rnn, 2 * d_conv, d_q, d_q, d_att, 3 * D)
    assert sum(splits) == w_in.shape[2] and 2 * d_qk == d_v
    offs = [0]
    for w_ in splits[:-1]:
        offs.append(offs[-1] + w_)
    order = (6, 2, 0, 1, 3, 4, 5)
    col = {}
    acc = 0
    for idx in order:
        col[idx] = acc
        acc += splits[idx]
    col_x, col_y, col_g, col_q, col_k, col_v, col_z = (col[i] for i in range(7))

    tl = _tiles(T)
    tcx = _tiles(Tc)
    tc_rnn = min(tl["tc_rnn"], d_rnn)

    R = -(-(B + 1) // SUBLANES) * SUBLANES
    cc = jnp.zeros((R, D), F32).at[:B].set(c).at[B].set(c_ctx)
    mods_all = _ada(cc, ada_w, ada_b).reshape(depth, R * N_MOD, 1, D)

    cs_tab, sn_tab = _rope_tables(T, d_qk)

    def lat_row(tm):
        per = T // tm
        return lambda i: i // per

    def ctx_row(tm):
        return lambda i: B

    xl = x.reshape(B * T, D)
    xc = ctx.reshape(B * Tc, D)
    for l in range(depth):
        need_ctx = l < depth - 1
        last = l == depth - 1
        lam_init = 0.8 - 0.6 * math.exp(-0.3 * l)
        mods = mods_all[l]
        wg = ffn_w_gate[l].astype(BF16)
        wu = ffn_w_up[l].astype(BF16)
        wd = ffn_w_down[l].astype(BF16)
        w_in_l = jnp.concatenate([w_in[l][:, offs[i]:offs[i] + splits[i]] for i in order], axis=1).astype(BF16)
        wr_bd = _block_diag(rg_w_r[l], tc_rnn)
        wi_bd = _block_diag(rg_w_i[l], tc_rnn)
        w_rnn = rnn_w_out[l].astype(BF16)
        w_cv = cv_w_out[l].astype(BF16)
        w_da = da_w_o[l].astype(BF16)
        w_o = w_out[l].astype(BF16)

        xl = _ffn(xl, mods, lat_row(tl["tm_ffn"]), 0, norm_g[l, 0], wg[0], wu[0], wd[0], tl["tm_ffn"], tl["tf"])
        xc = _ffn(xc, mods, ctx_row(tcx["tm_ffn"]), 0, norm_g[l, 0], wg[0], wu[0], wd[0], tcx["tm_ffn"], tcx["tf"])

        zl = _in_proj(xl, mods, lat_row(tl["tm_in"]), norm_g[l, 1], w_in_l, tl["tm_in"], tl["tn_in"])
        zc = _in_proj(xc, mods, ctx_row(tcx["tm_in"]), norm_g[l, 1], w_in_l, tcx["tm_in"], tcx["tn_in"])
        zl3 = zl.reshape(B, T, -1)
        zc3 = zc.reshape(B, Tc, -1)

        rnn_l, rnn_c = _rnn(zl3, zc3, col_x, col_y, rnn_conv_w[l], rnn_conv_b[l], wr_bd, rg_b_r[l], wi_bd,
                            rg_b_i[l], rg_lam[l], need_ctx, tc_rnn)
        conv_l = _conformer(zl3, col_g, cv_dw_w[l], cv_dw_b[l], cv_ln_g[l], cv_ln_b[l], tl["tt_conv"])
        att_l = _attention(zl3, [zl3, zc3], col_q, col_k, col_v, (cs_tab, sn_tab), da_lam[l], da_subln_g[l],
                           lam_init, n_heads, tl["tq"])
        xl = _merge(rnn_l.reshape(B * T, -1), conv_l.reshape(B * T, -1), att_l.reshape(B * T, -1), zl, col_z,
                    xl, mods, lat_row(tl["tm_merge"]), w_rnn, w_cv, w_da, w_o, tl["tm_merge"])
        xl = _ffn(xl, mods, lat_row(tl["tm_ffn"]), 2, norm_g[l, 2], wg[1], wu[1], wd[1], tl["tm_ffn"], tl["tf"],
                  final_g=final_g if last else None)
        if need_ctx:
            conv_c = _conformer(zc3, col_g, cv_dw_w[l], cv_dw_b[l], cv_ln_g[l], cv_ln_b[l], tcx["tt_conv"])
            att_c = _attention(zc3, [zc3], col_q, col_k, col_v, None, da_lam[l], da_subln_g[l],
                               lam_init, n_heads, tcx["tq"])
            xc = _merge(rnn_c.reshape(B * Tc, -1), conv_c.reshape(B * Tc, -1), att_c.reshape(B * Tc, -1), zc, col_z,
                        xc, mods, ctx_row(tcx["tm_merge"]), w_rnn, w_cv, w_da, w_o, tcx["tm_merge"])
            xc = _ffn(xc, mods, ctx_row(tcx["tm_ffn"]), 2, norm_g[l, 2], wg[1], wu[1], wd[1],
                      tcx["tm_ffn"], tcx["tf"])
    return xl.reshape(B, T, D)
```

```python
import functools
import math

import jax
import jax.numpy as jnp
from jax import lax
from jax.experimental import pallas as pl
from jax.experimental.pallas import tpu as pltpu

F32 = jnp.float32
BF16 = jnp.bfloat16

EPS = 1e-6
GRID_W = 64
ROPE_BASE = 10000.0
RG_C = 8.0
MACARON_W = 0.5
N_MOD = 9

LANES = 128
SUBLANES = 8
MXU_WIDTH = 256
VMEM_LIMIT = 56 * 1024 * 1024


def _tiles(T):
    return dict(
        tm_ffn=min(512, T), tf=512, tm_in=min(1024, T), tn_in=1024, tm_merge=min(256, T),
        tt_conv=min(256, T), tq=min(1024, T), tc_rnn=256)


def _params(sem):
    return pltpu.CompilerParams(dimension_semantics=sem, vmem_limit_bytes=VMEM_LIMIT)


def _mod_spec(row_fn, k, D):
    return pl.BlockSpec((None, 1, D), lambda i, *_: (row_fn(i) * N_MOD + k, 0, 0))


def _rms_mod(x, ng, sc, sh):
    ms = jnp.mean(x * x, axis=-1, keepdims=True)
    return (x * lax.rsqrt(ms + EPS) * ng) * (1.0 + sc) + sh


def _sigmoid(x):
    return 0.5 * jnp.tanh(0.5 * x) + 0.5


def _ada_kernel(c_ref, w_ref, b_ref, o_ref):
    c = c_ref[...]
    a = (c * _sigmoid(c)).astype(BF16)
    o_ref[...] = jnp.dot(a, w_ref[...].astype(BF16), preferred_element_type=F32) + b_ref[...]


def _ada(cc, ada_w, ada_b):
    depth, D, N = ada_w.shape
    R = cc.shape[0]
    tn = min(1024, D)
    assert N % tn == 0
    return pl.pallas_call(
        _ada_kernel,
        grid=(depth, N // tn),
        in_specs=[pl.BlockSpec((R, D), lambda l, j: (0, 0)),
                  pl.BlockSpec((None, D, tn), lambda l, j: (l, 0, j)),
                  pl.BlockSpec((None, 1, tn), lambda l, j: (l, 0, j))],
        out_specs=pl.BlockSpec((None, R, tn), lambda l, j: (l, 0, j)),
        out_shape=jax.ShapeDtypeStruct((depth, R, N), F32),
        compiler_params=_params(("arbitrary", "arbitrary")),
        name="ada_mods",
    )(cc, ada_w, ada_b.reshape(depth, 1, N))


def _ffn_kernel(*refs, final):
    if final:
        s_ref, sh_ref, sc_ref, gt_ref, ng_ref, wg_ref, wu_ref, wd_ref, fg_ref, o_ref, h_scr = refs
    else:
        s_ref, sh_ref, sc_ref, gt_ref, ng_ref, wg_ref, wu_ref, wd_ref, o_ref, h_scr = refs
    j = pl.program_id(1)

    @pl.when(j == 0)
    def _():
        h_scr[...] = _rms_mod(s_ref[...], ng_ref[...], sc_ref[...], sh_ref[...]).astype(BF16)
        o_ref[...] = jnp.zeros_like(o_ref)

    h = h_scr[...]
    g = jnp.dot(h, wg_ref[...], preferred_element_type=F32)
    u = jnp.dot(h, wu_ref[...], preferred_element_type=F32)
    a = (g * _sigmoid(g) * u).astype(BF16)
    o_ref[...] += jnp.dot(a, wd_ref[...], preferred_element_type=F32)

    @pl.when(j == pl.num_programs(1) - 1)
    def _():
        y = s_ref[...] + (MACARON_W * gt_ref[...]) * o_ref[...]
        if final:
            y = y * lax.rsqrt(jnp.mean(y * y, axis=-1, keepdims=True) + EPS) * fg_ref[...]
        o_ref[...] = y


def _ffn(s, mods, row_fn, k, ng, wg, wu, wd, l, which, tm, tf, final_g=None):
    M, D = s.shape
    Fd = wg.shape[3]
    assert M % tm == 0 and Fd % tf == 0
    in_specs = [pl.BlockSpec((tm, D), lambda i, j: (i, 0)),
                _mod_spec(row_fn, 3 * k, D), _mod_spec(row_fn, 3 * k + 1, D), _mod_spec(row_fn, 3 * k + 2, D),
                pl.BlockSpec((1, D), lambda i, j: (0, 0)),
                pl.BlockSpec((None, None, D, tf), lambda i, j: (l, which, 0, j)),
                pl.BlockSpec((None, None, D, tf), lambda i, j: (l, which, 0, j)),
                pl.BlockSpec((None, None, tf, D), lambda i, j: (l, which, j, 0))]
    args = [s, mods, mods, mods, ng.reshape(1, D), wg, wu, wd]
    if final_g is not None:
        in_specs.append(pl.BlockSpec((1, D), lambda i, j: (0, 0)))
        args.append(final_g.reshape(1, D))
    return pl.pallas_call(
        functools.partial(_ffn_kernel, final=final_g is not None),
        grid=(M // tm, Fd // tf),
        in_specs=in_specs,
        out_specs=pl.BlockSpec((tm, D), lambda i, j: (i, 0)),
        out_shape=jax.ShapeDtypeStruct((M, D), F32),
        scratch_shapes=[pltpu.VMEM((tm, D), BF16)],
        compiler_params=_params(("parallel", "arbitrary")),
        name="ffn_half_step",
    )(*args)


def _in_kernel(*refs, prefetch):
    if prefetch:
        refs = refs[1:]
    x_ref, sh_ref, sc_ref, ng_ref, w_ref, o_ref, h_scr = refs

    @pl.when(pl.program_id(1) == 0)
    def _():
        h_scr[...] = _rms_mod(x_ref[...], ng_ref[...], sc_ref[...], sh_ref[...]).astype(BF16)

    o_ref[...] = jnp.dot(h_scr[...], w_ref[...], preferred_element_type=F32).astype(BF16)


def _in_proj(s, mods, row_fn, ng, w, l, tm, tn, col_blocks=None):
    M, D = s.shape
    N = w.shape[2]
    assert M % tm == 0 and N % tn == 0
    nb = N // tn if col_blocks is None else len(col_blocks)
    prefetch = col_blocks is not None
    if prefetch:
        w_map = lambda i, j, cb: (l, 0, cb[j])
    else:
        w_map = lambda i, j: (l, 0, j)
    grid_spec = pltpu.PrefetchScalarGridSpec(
        num_scalar_prefetch=1 if prefetch else 0,
        grid=(M // tm, nb),
        in_specs=[pl.BlockSpec((tm, D), lambda i, j, *_: (i, 0)),
                  _mod_spec(row_fn, 3, D), _mod_spec(row_fn, 4, D),
                  pl.BlockSpec((1, D), lambda i, j, *_: (0, 0)),
                  pl.BlockSpec((None, D, tn), w_map)],
        out_specs=pl.BlockSpec((tm, tn), lambda i, j, *_: (i, j)),
        scratch_shapes=[pltpu.VMEM((tm, D), BF16)])
    args = [s, mods, mods, ng.reshape(1, D), w]
    if prefetch:
        args = [jnp.asarray(col_blocks, jnp.int32)] + args
    return pl.pallas_call(
        functools.partial(_in_kernel, prefetch=prefetch),
        grid_spec=grid_spec,
        out_shape=jax.ShapeDtypeStruct((M, nb * tn), BF16),
        compiler_params=_params(("parallel", "arbitrary")),
        name="in_proj",
    )(*args)


def _prefix8(A, Bv, row, reverse):
    for s in (1, 2, 4):
        if reverse:
            valid = row < SUBLANES - s
            sh = SUBLANES - s
        else:
            valid = row >= s
            sh = s
        As = jnp.where(valid, pltpu.roll(A, sh, 0), 1.0)
        Bs = jnp.where(valid, pltpu.roll(Bv, sh, 0), 0.0)
        Bv = A * Bs + Bv
        A = A * As
    return A, Bv


def _scan_both(a_scr, b_scr, hf_scr, hb_scr, start, n, hf, hb):
    C = hf_scr.shape[1]
    row = lax.broadcasted_iota(jnp.int32, (SUBLANES, C), 0)
    nblk = n // SUBLANES

    def body(i, carry):
        hf, hb = carry
        bf = pl.multiple_of(start + i * SUBLANES, SUBLANES)
        bb = pl.multiple_of(start + (nblk - 1 - i) * SUBLANES, SUBLANES)
        Af, Bf = _prefix8(a_scr[0, pl.ds(bf, SUBLANES), :], b_scr[0, pl.ds(bf, SUBLANES), :], row, False)
        Ab, Bb = _prefix8(a_scr[1, pl.ds(bb, SUBLANES), :], b_scr[1, pl.ds(bb, SUBLANES), :], row, True)
        Hf = Bf + Af * hf
        Hb = Bb + Ab * hb
        hf_scr[pl.ds(bf, SUBLANES), :] = Hf
        hb_scr[pl.ds(bb, SUBLANES), :] = Hb
        return (jnp.broadcast_to(Hf[SUBLANES - 1:SUBLANES, :], (SUBLANES, C)),
                jnp.broadcast_to(Hb[0:1, :], (SUBLANES, C)))

    return lax.fori_loop(0, nblk, body, (hf, hb), unroll=2)


def _rnn_kernel(*refs, T, Tc, need_ctx):
    if need_ctx:
        (xl_ref, xc_ref, yl_ref, yc_ref, cw_ref, cb_ref, wr_ref, br_ref, wi_ref, bi_ref, lam_ref,
         ol_ref, oc_ref, xp_l, xp_c, u_scr, a_scr, b_scr, hf_scr, hb_scr) = refs
    else:
        (xl_ref, xc_ref, yl_ref, cw_ref, cb_ref, wr_ref, br_ref, wi_ref, bi_ref, lam_ref,
         ol_ref, xp_l, xp_c, u_scr, a_scr, b_scr, hf_scr, hb_scr) = refs
    C = u_scr.shape[1]
    K = cw_ref.shape[0]
    P = SUBLANES

    def conv(x_ref, xp, off, n):
        xp[0:P, :] = jnp.zeros((P, C), F32)
        xp[P + n:2 * P + n, :] = jnp.zeros((P, C), F32)
        xp[P:P + n, :] = x_ref[...].astype(F32)
        acc = jnp.broadcast_to(cb_ref[...], (n, C))
        lo = (K - 1) // 2
        for k in range(K):
            acc = acc + cw_ref[k:k + 1, :] * xp[pl.ds(P - lo + k, n), :]
        u_scr[off:off + n, :] = acc

    conv(xc_ref, xp_c, 0, Tc)
    conv(xl_ref, xp_l, Tc, T)

    for d in range(2):
        neg_c_sp = (-RG_C) * jax.nn.softplus(-lam_ref[d:d + 1, :])
        for off, n in ((0, Tc), (Tc, T)):
            uu = u_scr[off:off + n, :]
            ub = uu.astype(BF16)
            r = _sigmoid(jnp.dot(ub, wr_ref[d], preferred_element_type=F32) + br_ref[d:d + 1, :])
            ig = _sigmoid(jnp.dot(ub, wi_ref[d], preferred_element_type=F32) + bi_ref[d:d + 1, :])
            log_a = r * neg_c_sp
            a = jnp.exp(log_a)
            one_minus_a2 = -jnp.tanh(log_a) * (1.0 + a * a)
            a_scr[d, off:off + n, :] = a
            b_scr[d, off:off + n, :] = jnp.sqrt(one_minus_a2) * ig * uu

    zero = jnp.zeros((SUBLANES, C), F32)
    hf, hb = _scan_both(a_scr, b_scr, hf_scr, hb_scr, 0, Tc, zero, zero)
    _scan_both(a_scr, b_scr, hf_scr, hb_scr, Tc, T, hf, hb)

    ol_ref[...] = (jax.nn.gelu(yl_ref[...].astype(F32)) * (hf_scr[Tc:Tc + T, :] + hb_scr[Tc:Tc + T, :])).astype(BF16)
    if need_ctx:
        oc_ref[...] = (jax.nn.gelu(yc_ref[...].astype(F32)) * (hf_scr[0:Tc, :] + hb_scr[0:Tc, :])).astype(BF16)


def _rnn(zl, zc, col_x, col_y, col_xc, col_yc, cw, cb, wr_bd, br, wi_bd, bi, lam, l, need_ctx, tc):
    B, T, _ = zl.shape
    Tc = zc.shape[1]
    K, C = cw.shape[1:]
    assert C % tc == 0 and T % SUBLANES == 0 and Tc % SUBLANES == 0
    assert all(o % tc == 0 for o in (col_x, col_y, col_xc)) and (col_yc is None or col_yc % tc == 0)
    nt = C // tc
    bx, by, bxc = col_x // tc, col_y // tc, col_xc // tc
    in_specs = [pl.BlockSpec((None, T, tc), lambda b, c: (b, 0, bx + c)),
                pl.BlockSpec((None, Tc, tc), lambda b, c: (b, 0, bxc + c)),
                pl.BlockSpec((None, T, tc), lambda b, c: (b, 0, by + c))]
    args = [zl, zc, zl]
    if need_ctx:
        byc = col_yc // tc
        in_specs.append(pl.BlockSpec((None, Tc, tc), lambda b, c: (b, 0, byc + c)))
        args.append(zc)
    in_specs += [pl.BlockSpec((None, K, tc), lambda b, c: (l, 0, c)),
                 pl.BlockSpec((None, 1, tc), lambda b, c: (l, 0, c)),
                 pl.BlockSpec((None, 2, None, tc, tc), lambda b, c: (l, 0, c, 0, 0)),
                 pl.BlockSpec((None, 2, tc), lambda b, c: (l, 0, c)),
                 pl.BlockSpec((None, 2, None, tc, tc), lambda b, c: (l, 0, c, 0, 0)),
                 pl.BlockSpec((None, 2, tc), lambda b, c: (l, 0, c)),
                 pl.BlockSpec((None, 2, tc), lambda b, c: (l, 0, c))]
    args += [cw, cb.reshape(cb.shape[0], 1, C), wr_bd, br, wi_bd, bi, lam]
    out_specs = [pl.BlockSpec((None, T, tc), lambda b, c: (b, 0, c))]
    out_shape = [jax.ShapeDtypeStruct((B, T, C), BF16)]
    if need_ctx:
        out_specs.append(pl.BlockSpec((None, Tc, tc), lambda b, c: (b, 0, c)))
        out_shape.append(jax.ShapeDtypeStruct((B, Tc, C), BF16))
    Tt = T + Tc
    outs = pl.pallas_call(
        functools.partial(_rnn_kernel, T=T, Tc=Tc, need_ctx=need_ctx),
        grid=(B, nt),
        in_specs=in_specs,
        out_specs=out_specs,
        out_shape=out_shape,
        scratch_shapes=[pltpu.VMEM((T + 2 * SUBLANES, tc), F32), pltpu.VMEM((Tc + 2 * SUBLANES, tc), F32),
                        pltpu.VMEM((Tt, tc), F32), pltpu.VMEM((2, Tt, tc), F32), pltpu.VMEM((2, Tt, tc), F32),
                        pltpu.VMEM((Tt, tc), F32), pltpu.VMEM((Tt, tc), F32)],
        compiler_params=_params(("parallel", "parallel")),
        name="rglru",
    )(*args)
    return (outs[0], outs[1]) if need_ctx else (outs[0], None)


def _conv_kernel(cur_ref, prev_ref, next_ref, w_ref, b_ref, lg_ref, lb_ref, o_ref, xp, sh_scr, cv_scr, *, halo, rb, lt):
    i = pl.program_id(1)
    tt, C = o_ref.shape
    K = w_ref.shape[0]
    half = (K - 1) // 2

    def glu(ref):
        g = ref[...].astype(F32)
        return g[:, :C] * _sigmoid(g[:, C:])

    xp[0:halo, :] = jnp.where(i > 0, glu(prev_ref), 0.0)
    xp[halo:halo + tt, :] = glu(cur_ref)
    xp[halo + tt:2 * halo + tt, :] = jnp.where(i < pl.num_programs(1) - 1, glu(next_ref), 0.0)

    n_sh = tt + 2 * halo - SUBLANES
    for r in range(SUBLANES):
        sh_scr[r] = xp[pl.ds(r, n_sh), :]

    def blk(bi, carry):
        for li in range(C // lt):
            ls = slice(li * lt, (li + 1) * lt)
            acc = jnp.broadcast_to(b_ref[:, ls], (rb, lt))
            for k in range(K):
                q, r = divmod(halo - half + k, SUBLANES)
                base = pl.multiple_of(bi * rb + q * SUBLANES, SUBLANES)
                acc = acc + w_ref[k:k + 1, ls] * sh_scr[r, pl.ds(base, rb), ls]
            cv_scr[pl.ds(pl.multiple_of(bi * rb, SUBLANES), rb), ls] = acc
        return carry

    lax.fori_loop(0, tt // rb, blk, 0)

    u = cv_scr[...]
    mu = jnp.mean(u, axis=-1, keepdims=True)
    uc = u - mu
    var = jnp.mean(uc * uc, axis=-1, keepdims=True)
    y = uc * lax.rsqrt(var + EPS) * lg_ref[...] + lb_ref[...]
    o_ref[...] = (y * _sigmoid(y)).astype(BF16)


def _conformer(z, col_g, w, b, lg, lb, l, tt):
    B, T, _ = z.shape
    K, C = w.shape[1:]
    halo = 16
    assert (K - 1) // 2 <= halo and T % tt == 0 and tt % halo == 0 and col_g % (2 * C) == 0
    bg = col_g // (2 * C)
    th = tt // halo
    nh = T // halo
    rb = min(32, tt)
    lt = min(256, C)
    vec = lambda v: v.reshape(v.shape[0], 1, C)
    vspec = pl.BlockSpec((None, 1, C), lambda b_, i: (l, 0, 0))
    return pl.pallas_call(
        functools.partial(_conv_kernel, halo=halo, rb=rb, lt=lt),
        grid=(B, T // tt),
        in_specs=[pl.BlockSpec((None, tt, 2 * C), lambda b_, i: (b_, i, bg)),
                  pl.BlockSpec((None, halo, 2 * C), lambda b_, i: (b_, jnp.maximum(i * th - 1, 0), bg)),
                  pl.BlockSpec((None, halo, 2 * C), lambda b_, i: (b_, jnp.minimum((i + 1) * th, nh - 1), bg)),
                  pl.BlockSpec((None, K, C), lambda b_, i: (l, 0, 0)),
                  vspec, vspec, vspec],
        out_specs=pl.BlockSpec((None, tt, C), lambda b_, i: (b_, i, 0)),
        out_shape=jax.ShapeDtypeStruct((B, T, C), BF16),
        scratch_shapes=[pltpu.VMEM((tt + 2 * halo, C), F32),
                        pltpu.VMEM((SUBLANES, tt + 2 * halo - SUBLANES, C), F32),
                        pltpu.VMEM((tt, C), F32)],
        compiler_params=_params(("parallel", "parallel")),
        name="conformer_conv",
    )(z, z, z, w, vec(b), vec(lg), vec(lb))


def _rope(x, cs, sn, quarter):
    lane = lax.broadcasted_iota(jnp.int32, x.shape, 1)
    partner = jnp.where(lane % (2 * quarter) < quarter,
                        pltpu.roll(x, x.shape[1] - quarter, 1), pltpu.roll(x, quarter, 1))
    return x * cs + partner * sn


def _attn_kernel(*refs, n_src, rope, lam_init, scale, hd, quarter, rs):
    it = iter(refs)
    q_ref = next(it)
    kv = [(next(it), next(it)) for _ in range(n_src)]
    if rope:
        cq_ref, sq_ref, ck_ref, sk_ref = next(it), next(it), next(it), next(it)
    dl_ref, g_ref, o_ref, k_all, v_ext = next(it), next(it), next(it), next(it), next(it)
    i = pl.program_id(2)
    tq, W = q_ref.shape
    hp = W // hd

    @pl.when(i == 0)
    def _():
        off = 0
        for s in range(n_src):
            k_ref, v_ref = kv[s]
            n = k_ref.shape[0]
            if rope and s == 0:
                k_all[off:off + n, :] = _rope(k_ref[...].astype(F32), ck_ref[...], sk_ref[...], quarter).astype(BF16)
            else:
                k_all[off:off + n, :] = k_ref[...]
            for h in range(hp):
                v_ext[h, off:off + n, 0:hd] = v_ref[:, h * hd:(h + 1) * hd]
                v_ext[h, off:off + n, hd:2 * hd] = jnp.ones((n, hd), BF16)
            off += n

    q = q_ref[...].astype(F32)
    if rope:
        q = _rope(q, cq_ref[...], sq_ref[...], quarter)
    qb = (q * scale).astype(BF16)
    keys = k_all[...]

    lq = dl_ref[...]
    lam = (jnp.exp(jnp.sum(lq[0:1] * lq[1:2], axis=-1, keepdims=True))
           - jnp.exp(jnp.sum(lq[2:3] * lq[3:4], axis=-1, keepdims=True)) + lam_init)

    lane = lax.broadcasted_iota(jnp.int32, (rs, W), 1)
    half = hd // 2

    def scores(r, ch):
        lo = ch * half
        qr = qb[r * rs:(r + 1) * rs, :]
        qc = jnp.where((lane >= lo) & (lane < lo + half), qr, jnp.zeros_like(qr))
        return lax.dot_general(qc, keys, (((1,), (1,)), ((), ())), preferred_element_type=F32)

    def weights(s):
        return jnp.exp(s - jnp.max(s, axis=-1, keepdims=True)).astype(BF16)

    def attend(ch, e):
        ne = jnp.dot(e, v_ext[ch // 2], preferred_element_type=F32)
        return ne[:, 0:hd] / ne[:, hd:2 * hd]

    chains = [(r, ch) for r in range(tq // rs) for ch in range(2 * hp)]
    s_cur = scores(*chains[0])
    e_prev = None
    comps = []
    for n in range(len(chains)):
        s_next = scores(*chains[n + 1]) if n + 1 < len(chains) else None
        e_cur = weights(s_cur)
        if e_prev is not None:
            comps.append(attend(chains[n - 1][1], e_prev))
        e_prev, s_cur = e_cur, s_next
    comps.append(attend(chains[-1][1], e_prev))

    for n in range(0, len(chains), 2):
        r, h = chains[n][0], chains[n][1] // 2
        o = comps[n] - lam * comps[n + 1]
        y = o * lax.rsqrt(jnp.mean(o * o, axis=-1, keepdims=True) + EPS) * g_ref[...] * (1.0 - lam_init)
        o_ref[r * rs:(r + 1) * rs, h * hd:(h + 1) * hd] = y.astype(BF16)


def _attention(zq, srcs, col_q, col_ks, col_vs, tabs, da_lam, subln_g, l, lam_init, n_heads, tq):
    B, Tq, _ = zq.shape
    hd = subln_g.shape[1]
    W = min(MXU_WIDTH, n_heads * hd)
    hp = W // hd
    assert hd == LANES and n_heads % hp == 0 and col_q % W == 0 and Tq % tq == 0
    assert all(ck % W == 0 for ck in col_ks) and all(cv % W == 0 for cv in col_vs)
    bq = col_q // W
    rope = tabs is not None
    in_specs = [pl.BlockSpec((None, tq, W), lambda b, h, i: (b, i, bq + h))]
    args = [zq]
    Tk_all = 0
    for zs, ck, cv in zip(srcs, col_ks, col_vs):
        Tk = zs.shape[1]
        Tk_all += Tk
        bk, bv = ck // W, cv // W
        in_specs += [pl.BlockSpec((None, Tk, W), lambda b, h, i, bk=bk: (b, 0, bk + h)),
                     pl.BlockSpec((None, Tk, W), lambda b, h, i, bv=bv: (b, 0, bv + h))]
        args += [zs, zs]
    if rope:
        cs, sn = tabs
        in_specs += [pl.BlockSpec((tq, W), lambda b, h, i: (i, 0)), pl.BlockSpec((tq, W), lambda b, h, i: (i, 0)),
                     pl.BlockSpec((Tq, W), lambda b, h, i: (0, 0)), pl.BlockSpec((Tq, W), lambda b, h, i: (0, 0))]
        args += [cs, sn, cs, sn]
    d_qk = da_lam.shape[2]
    in_specs += [pl.BlockSpec((None,) + da_lam.shape[1:], lambda b, h, i: (l, 0, 0)),
                 pl.BlockSpec((None, 1, hd), lambda b, h, i: (l, 0, 0))]
    args += [da_lam, subln_g.reshape(subln_g.shape[0], 1, hd)]
    return pl.pallas_call(
        functools.partial(_attn_kernel, n_src=len(srcs), rope=rope, lam_init=lam_init, scale=d_qk ** -0.5,
                          hd=hd, quarter=d_qk // 4, rs=min(tq, 256)),
        grid=(B, n_heads // hp, Tq // tq),
        in_specs=in_specs,
        out_specs=pl.BlockSpec((None, tq, W), lambda b, h, i: (b, i, h)),
        out_shape=jax.ShapeDtypeStruct((B, Tq, n_heads * hd), BF16),
        scratch_shapes=[pltpu.VMEM((Tk_all, W), BF16), pltpu.VMEM((hp, Tk_all, 2 * hd), BF16)],
        compiler_params=_params(("parallel", "parallel", "arbitrary")),
        name="diff_attention",
    )(*args)


def _merge_kernel(*refs, nsplit):
    r_ref, c_ref, a_ref = refs[0:3]
    z_refs = refs[3:3 + 3 * nsplit]
    x_ref, gt_ref, w0_ref, w1_ref, w2_ref, wo_ref, o_ref = refs[3 + 3 * nsplit:]
    gw = z_refs[0].shape[1]
    ins = (r_ref[...], c_ref[...], a_ref[...])
    ws = (w0_ref, w1_ref, w2_ref)
    out = None
    for p in range(nsplit):
        cols = slice(p * gw, (p + 1) * gw)
        m = None
        for j in range(3):
            gate = _sigmoid(z_refs[j * nsplit + p][...].astype(F32))
            t = gate * jnp.dot(ins[j], ws[j][:, cols], preferred_element_type=F32)
            m = t if m is None else m + t
        t = jnp.dot(m.astype(BF16), wo_ref[cols, :], preferred_element_type=F32)
        out = t if out is None else out + t
    o_ref[...] = x_ref[...] + gt_ref[...] * out


def _merge(rnn_g, conv_g, att_g, z, col_z, s, mods, row_fn, w0, w1, w2, wo, l, tm):
    M, D = s.shape
    gw = math.gcd(col_z, D)
    nsplit = D // gw
    bz = col_z // gw
    assert M % tm == 0 and gw % LANES == 0

    def act_spec(a):
        return pl.BlockSpec((tm, a.shape[1]), lambda i: (i, 0))

    def w_spec(w):
        return pl.BlockSpec((None,) + w.shape[1:], lambda i: (l, 0, 0), pipeline_mode=pl.Buffered(1))

    z_specs = [pl.BlockSpec((tm, gw), lambda i, blk=bz + j * nsplit + p: (i, blk))
               for j in range(3) for p in range(nsplit)]
    return pl.pallas_call(
        functools.partial(_merge_kernel, nsplit=nsplit),
        grid=(M // tm,),
        in_specs=[act_spec(rnn_g), act_spec(conv_g), act_spec(att_g)] + z_specs + [
            pl.BlockSpec((tm, D), lambda i: (i, 0)),
            _mod_spec(row_fn, 5, D),
            w_spec(w0), w_spec(w1), w_spec(w2), w_spec(wo)],
        out_specs=pl.BlockSpec((tm, D), lambda i: (i, 0)),
        out_shape=jax.ShapeDtypeStruct((M, D), F32),
        compiler_params=_params(("parallel",)),
        name="merge_out_proj",
    )(rnn_g, conv_g, att_g, *([z] * (3 * nsplit)), s, mods, w0, w1, w2, wo)


def _rope_tables(T, d_qk, width):
    n = d_qk // 4
    rows = T // GRID_W
    row = jnp.repeat(jnp.arange(rows, dtype=jnp.int32), GRID_W).astype(F32)
    col = jnp.tile(jnp.arange(GRID_W, dtype=jnp.int32), rows).astype(F32)
    inv = ROPE_BASE ** (-jnp.arange(n, dtype=F32) * 2.0 / (2 * n))
    ar, ac = row[:, None] * inv, col[:, None] * inv
    cs = jnp.concatenate([jnp.cos(ar), jnp.cos(ar), jnp.cos(ac), jnp.cos(ac)], axis=1)
    sn = jnp.concatenate([-jnp.sin(ar), jnp.sin(ar), -jnp.sin(ac), jnp.sin(ac)], axis=1)
    return jnp.tile(cs, (1, width // d_qk)), jnp.tile(sn, (1, width // d_qk))


def _block_diag(w, tc):
    depth, two, nb, bw, _ = w.shape
    per = tc // bw
    w6 = w.reshape(depth, two, nb // per, per, bw, bw)
    eye = jnp.eye(per, dtype=w.dtype)
    bd = jnp.einsum('ldtpij,pq->ldtpiqj', w6, eye)
    return bd.reshape(depth, two, nb // per, tc, tc).astype(BF16)


def kernel(x, c, ctx, c_ctx, ada_w, ada_b, norm_g, ffn_w_gate, ffn_w_up, ffn_w_down, w_in, rnn_conv_w, rnn_conv_b, rg_w_r, rg_b_r, rg_w_i, rg_b_i, rg_lam, rnn_w_out, cv_dw_w, cv_dw_b, cv_ln_g, cv_ln_b, cv_w_out, da_lam, da_subln_g, da_w_o, w_out, final_g):
    B, T, D = x.shape
    Tc = ctx.shape[1]
    depth = ada_w.shape[0]
    d_rnn = rnn_conv_w.shape[2]
    d_conv = cv_dw_w.shape[2]
    d_qk = da_lam.shape[2]
    d_v = da_subln_g.shape[1]
    d_att = da_w_o.shape[1]
    n_heads = d_att // d_v
    d_q = n_heads * 2 * d_qk
    splits = (d_rnn, d_rnn, 2 * d_conv, d_q, d_q, d_att, 3 * D)
    assert sum(splits) == w_in.shape[2] and 2 * d_qk == d_v
    offs = [0]
    for w_ in splits[:-1]:
        offs.append(offs[-1] + w_)
    col_x, col_y, col_g, col_q, col_k, col_v, col_z = offs

    tl = _tiles(T)
    tcx = _tiles(Tc)
    tc_rnn = min(tl["tc_rnn"], d_rnn)
    att_w = min(MXU_WIDTH, n_heads * d_v)
    tn_in = math.gcd(w_in.shape[2], tl["tn_in"])

    R = -(-(B + 1) // SUBLANES) * SUBLANES
    cc = jnp.zeros((R, D), F32).at[:B].set(c).at[B].set(c_ctx)
    mods_all = _ada(cc, ada_w, ada_b).reshape(depth, R * N_MOD, 1, D)

    tabs = _rope_tables(T, d_qk, att_w)

    wg, wu, wd = ffn_w_gate.astype(BF16), ffn_w_up.astype(BF16), ffn_w_down.astype(BF16)
    w_in_b = w_in.astype(BF16)
    wr_bd = _block_diag(rg_w_r, tc_rnn)
    wi_bd = _block_diag(rg_w_i, tc_rnn)
    w_rnn, w_cv, w_da, w_o = (rnn_w_out.astype(BF16), cv_w_out.astype(BF16), da_w_o.astype(BF16),
                              w_out.astype(BF16))

    tn_c = math.gcd(math.gcd(d_rnn, col_k), math.gcd(d_q, d_att))
    ctx_blocks = tuple(range(col_x // tn_c, (col_x + d_rnn) // tn_c)) + \
        tuple(range(col_k // tn_c, (col_v + d_att) // tn_c))

    def lat_row(tm):
        per = T // tm
        return lambda i: i // per

    def ctx_row(tm):
        return lambda i: B

    xl = x.reshape(B * T, D)
    xc = ctx.reshape(B * Tc, D)
    for l in range(depth):
        need_ctx = l < depth - 1
        last = l == depth - 1
        lam_init = 0.8 - 0.6 * math.exp(-0.3 * l)
        mods = mods_all[l]

        xl = _ffn(xl, mods, lat_row(tl["tm_ffn"]), 0, norm_g[l, 0], wg, wu, wd, l, 0, tl["tm_ffn"], tl["tf"])
        xc = _ffn(xc, mods, ctx_row(tcx["tm_ffn"]), 0, norm_g[l, 0], wg, wu, wd, l, 0, tcx["tm_ffn"], tcx["tf"])

        zl = _in_proj(xl, mods, lat_row(tl["tm_in"]), norm_g[l, 1], w_in_b, l, tl["tm_in"], tn_in)
        if need_ctx:
            zc = _in_proj(xc, mods, ctx_row(tcx["tm_in"]), norm_g[l, 1], w_in_b, l, tcx["tm_in"], tn_in)
            cxc, cyc, ckc, cvc = col_x, col_y, col_k, col_v
        else:
            zc = _in_proj(xc, mods, ctx_row(tcx["tm_in"]), norm_g[l, 1], w_in_b, l, tcx["tm_in"], tn_c,
                          col_blocks=ctx_blocks)
            cxc, cyc, ckc, cvc = 0, None, d_rnn, d_rnn + d_q
        zl3 = zl.reshape(B, T, -1)
        zc3 = zc.reshape(B, Tc, -1)

        rnn_l, rnn_c = _rnn(zl3, zc3, col_x, col_y, cxc, cyc, rnn_conv_w, rnn_conv_b, wr_bd, rg_b_r, wi_bd,
                            rg_b_i, rg_lam, l, need_ctx, tc_rnn)
        conv_l = _conformer(zl3, col_g, cv_dw_w, cv_dw_b, cv_ln_g, cv_ln_b, l, tl["tt_conv"])
        att_l = _attention(zl3, [zl3, zc3], col_q, [col_k, ckc], [col_v, cvc], tabs, da_lam, da_subln_g, l,
                           lam_init, n_heads, tl["tq"])
        xl = _merge(rnn_l.reshape(B * T, -1), conv_l.reshape(B * T, -1), att_l.reshape(B * T, -1), zl, col_z,
                    xl, mods, lat_row(tl["tm_merge"]), w_rnn, w_cv, w_da, w_o, l, tl["tm_merge"])
        xl = _ffn(xl, mods, lat_row(tl["tm_ffn"]), 2, norm_g[l, 2], wg, wu, wd, l, 1, tl["tm_ffn"], tl["tf"],
                  final_g=final_g if last else None)
        if need_ctx:
            conv_c = _conformer(zc3, col_g, cv_dw_w, cv_dw_b, cv_ln_g, cv_ln_b, l, tcx["tt_conv"])
            att_c = _attention(zc3, [zc3], col_q, [col_k], [col_v], None, da_lam, da_subln_g, l,
                               lam_init, n_heads, tcx["tq"])
            xc = _merge(rnn_c.reshape(B * Tc, -1), conv_c.reshape(B * Tc, -1), att_c.reshape(B * Tc, -1), zc, col_z,
                        xc, mods, ctx_row(tcx["tm_merge"]), w_rnn, w_cv, w_da, w_o, l, tcx["tm_merge"])
            xc = _ffn(xc, mods, ctx_row(tcx["tm_ffn"]), 2, norm_g[l, 2], wg, wu, wd, l, 1,
                      tcx["tm_ffn"], tcx["tf"])
    return xl.reshape(B, T, D)
```

```python
import functools
import math

import jax
import jax.numpy as jnp
from jax import lax
from jax.experimental import pallas as pl
from jax.experimental.pallas import tpu as pltpu

F32 = jnp.float32
BF16 = jnp.bfloat16

EPS = 1e-6
GRID_W = 64
ROPE_BASE = 10000.0
RG_C = 8.0
MACARON_W = 0.5
N_MOD = 9

LANES = 128
SUBLANES = 8
MXU_WIDTH = 256
VMEM_LIMIT = 56 * 1024 * 1024


def _tiles(T, rows):
    return dict(
        tm_ffn=min(512, rows), tf=512, tm_in=min(1024, rows), tm_merge=min(256, rows),
        tt_conv=min(256, T), tq=min(1024, T), tc_rnn=256)


def _col_tile(n):
    for unit in (MXU_WIDTH, LANES):
        fits = [t for t in range(unit, 1024 + 1, unit) if n % t == 0]
        if fits:
            return max(fits)
    raise ValueError(f"{n} columns are not lane aligned")


def _params(sem):
    return pltpu.CompilerParams(dimension_semantics=sem, vmem_limit_bytes=VMEM_LIMIT)


def _mod_spec(row_fn, k, D):
    return pl.BlockSpec((None, 1, D), lambda i, *_: (row_fn(i) * N_MOD + k, 0, 0))


def _rms_mod(x, ng, sc, sh):
    ms = jnp.mean(x * x, axis=-1, keepdims=True)
    return (x * lax.rsqrt(ms + EPS)) * (ng * (1.0 + sc)) + sh


def _sigmoid(x):
    return 0.5 * jnp.tanh(0.5 * x) + 0.5


def _ada_kernel(c_ref, w_ref, b_ref, o_ref):
    c = c_ref[...]
    a = (c * _sigmoid(c)).astype(BF16)
    o_ref[...] = jnp.dot(a, w_ref[...].astype(BF16), preferred_element_type=F32) + b_ref[...]


def _ada(cc, ada_w, ada_b):
    depth, D, N = ada_w.shape
    R = cc.shape[0]
    tn = min(1024, D)
    assert N % tn == 0
    return pl.pallas_call(
        _ada_kernel,
        grid=(depth, N // tn),
        in_specs=[pl.BlockSpec((R, D), lambda l, j: (0, 0)),
                  pl.BlockSpec((None, D, tn), lambda l, j: (l, 0, j)),
                  pl.BlockSpec((None, 1, tn), lambda l, j: (l, 0, j))],
        out_specs=pl.BlockSpec((None, R, tn), lambda l, j: (l, 0, j)),
        out_shape=jax.ShapeDtypeStruct((depth, R, N), F32),
        compiler_params=_params(("arbitrary", "arbitrary")),
        name="ada_mods",
    )(cc, ada_w, ada_b.reshape(depth, 1, N))


def _ffn_kernel(*refs, final):
    if final:
        s_ref, sh_ref, sc_ref, gt_ref, ng_ref, wg_ref, wu_ref, wd_ref, fg_ref, o_ref, h_scr = refs
    else:
        s_ref, sh_ref, sc_ref, gt_ref, ng_ref, wg_ref, wu_ref, wd_ref, o_ref, h_scr = refs
    j = pl.program_id(1)

    @pl.when(j == 0)
    def _():
        h_scr[...] = _rms_mod(s_ref[...], ng_ref[...], sc_ref[...], sh_ref[...]).astype(BF16)
        o_ref[...] = jnp.zeros_like(o_ref)

    h = h_scr[...]
    g = jnp.dot(h, wg_ref[...], preferred_element_type=F32)
    u = jnp.dot(h, wu_ref[...], preferred_element_type=F32)
    a = (g * _sigmoid(g) * u).astype(BF16)
    o_ref[...] += jnp.dot(a, wd_ref[...], preferred_element_type=F32)

    @pl.when(j == pl.num_programs(1) - 1)
    def _():
        y = s_ref[...] + (MACARON_W * gt_ref[...]) * o_ref[...]
        if final:
            y = y * lax.rsqrt(jnp.mean(y * y, axis=-1, keepdims=True) + EPS) * fg_ref[...]
        o_ref[...] = y


def _ffn(s, mods, row_fn, k, ng, wg, wu, wd, l, which, tm, tf, final_g=None):
    M, D = s.shape
    Fd = wg.shape[3]
    assert M % tm == 0 and Fd % tf == 0
    in_specs = [pl.BlockSpec((tm, D), lambda i, j: (i, 0)),
                _mod_spec(row_fn, 3 * k, D), _mod_spec(row_fn, 3 * k + 1, D), _mod_spec(row_fn, 3 * k + 2, D),
                pl.BlockSpec((1, D), lambda i, j: (0, 0)),
                pl.BlockSpec((None, None, D, tf), lambda i, j: (l, which, 0, j)),
                pl.BlockSpec((None, None, D, tf), lambda i, j: (l, which, 0, j)),
                pl.BlockSpec((None, None, tf, D), lambda i, j: (l, which, j, 0))]
    args = [s, mods, mods, mods, ng.reshape(1, D), wg, wu, wd]
    if final_g is not None:
        in_specs.append(pl.BlockSpec((1, D), lambda i, j: (0, 0)))
        args.append(final_g.reshape(1, D))
    return pl.pallas_call(
        functools.partial(_ffn_kernel, final=final_g is not None),
        grid=(M // tm, Fd // tf),
        in_specs=in_specs,
        out_specs=pl.BlockSpec((tm, D), lambda i, j: (i, 0)),
        out_shape=jax.ShapeDtypeStruct((M, D), F32),
        scratch_shapes=[pltpu.VMEM((tm, D), BF16)],
        compiler_params=_params(("parallel", "arbitrary")),
        name="ffn_half_step",
    )(*args)


def _in_kernel(*refs, prefetch):
    if prefetch:
        refs = refs[1:]
    x_ref, sh_ref, sc_ref, ng_ref, w_ref, o_ref, h_scr = refs

    @pl.when(pl.program_id(1) == 0)
    def _():
        h_scr[...] = _rms_mod(x_ref[...], ng_ref[...], sc_ref[...], sh_ref[...]).astype(BF16)

    o_ref[...] = jnp.dot(h_scr[...], w_ref[...], preferred_element_type=F32).astype(BF16)


def _in_proj(s, mods, row_fn, ng, w, l, tm, tn, col_blocks=None):
    M, D = s.shape
    N = w.shape[2]
    assert M % tm == 0 and N % tn == 0
    nb = N // tn if col_blocks is None else len(col_blocks)
    prefetch = col_blocks is not None
    if prefetch:
        w_map = lambda i, j, cb: (l, 0, cb[j])
    else:
        w_map = lambda i, j: (l, 0, j)
    grid_spec = pltpu.PrefetchScalarGridSpec(
        num_scalar_prefetch=1 if prefetch else 0,
        grid=(M // tm, nb),
        in_specs=[pl.BlockSpec((tm, D), lambda i, j, *_: (i, 0)),
                  _mod_spec(row_fn, 3, D), _mod_spec(row_fn, 4, D),
                  pl.BlockSpec((1, D), lambda i, j, *_: (0, 0)),
                  pl.BlockSpec((None, D, tn), w_map)],
        out_specs=pl.BlockSpec((tm, tn), lambda i, j, *_: (i, j)),
        scratch_shapes=[pltpu.VMEM((tm, D), BF16)])
    args = [s, mods, mods, ng.reshape(1, D), w]
    if prefetch:
        args = [jnp.asarray(col_blocks, jnp.int32)] + args
    return pl.pallas_call(
        functools.partial(_in_kernel, prefetch=prefetch),
        grid_spec=grid_spec,
        out_shape=jax.ShapeDtypeStruct((M, nb * tn), BF16),
        compiler_params=_params(("parallel", "arbitrary")),
        name="in_proj",
    )(*args)


def _prefix8(A, Bv, row, reverse):
    for s in (1, 2, 4):
        if reverse:
            valid = row < SUBLANES - s
            sh = SUBLANES - s
        else:
            valid = row >= s
            sh = s
        As = jnp.where(valid, pltpu.roll(A, sh, 0), 1.0)
        Bs = jnp.where(valid, pltpu.roll(Bv, sh, 0), 0.0)
        Bv = A * Bs + Bv
        A = A * As
    return A, Bv


def _scan_both(a_scr, b_scr, hf_scr, hb_scr, start, n, hf, hb):
    C = hf_scr.shape[1]
    row = lax.broadcasted_iota(jnp.int32, (SUBLANES, C), 0)
    nblk = n // SUBLANES

    def body(i, carry):
        hf, hb = carry
        bf = pl.multiple_of(start + i * SUBLANES, SUBLANES)
        bb = pl.multiple_of(start + (nblk - 1 - i) * SUBLANES, SUBLANES)
        Af, Bf = _prefix8(a_scr[0, pl.ds(bf, SUBLANES), :], b_scr[0, pl.ds(bf, SUBLANES), :], row, False)
        Ab, Bb = _prefix8(a_scr[1, pl.ds(bb, SUBLANES), :], b_scr[1, pl.ds(bb, SUBLANES), :], row, True)
        Hf = Bf + Af * hf
        Hb = Bb + Ab * hb
        hf_scr[pl.ds(bf, SUBLANES), :] = Hf
        hb_scr[pl.ds(bb, SUBLANES), :] = Hb
        return (jnp.broadcast_to(Hf[SUBLANES - 1:SUBLANES, :], (SUBLANES, C)),
                jnp.broadcast_to(Hb[0:1, :], (SUBLANES, C)))

    return lax.fori_loop(0, nblk, body, (hf, hb), unroll=2)


def _rnn_kernel(*refs, T, Tc, need_ctx):
    if need_ctx:
        (xl_ref, xc_ref, yl_ref, yc_ref, cw_ref, cb_ref, wr_ref, br_ref, wi_ref, bi_ref, lam_ref,
         ol_ref, oc_ref, xp_l, xp_c, u_scr, a_scr, b_scr, hf_scr, hb_scr) = refs
    else:
        (xl_ref, xc_ref, yl_ref, cw_ref, cb_ref, wr_ref, br_ref, wi_ref, bi_ref, lam_ref,
         ol_ref, xp_l, xp_c, u_scr, a_scr, b_scr, hf_scr, hb_scr) = refs
    C = u_scr.shape[1]
    K = cw_ref.shape[0]
    P = SUBLANES

    def conv(x_ref, xp, off, n):
        xp[0:P, :] = jnp.zeros((P, C), F32)
        xp[P + n:2 * P + n, :] = jnp.zeros((P, C), F32)
        xp[P:P + n, :] = x_ref[...].astype(F32)
        acc = jnp.broadcast_to(cb_ref[...], (n, C))
        lo = (K - 1) // 2
        for k in range(K):
            acc = acc + cw_ref[k:k + 1, :] * xp[pl.ds(P - lo + k, n), :]
        u_scr[off:off + n, :] = acc

    conv(xc_ref, xp_c, 0, Tc)
    conv(xl_ref, xp_l, Tc, T)

    tiny = jnp.finfo(F32).tiny
    for d in range(2):
        hk = (-0.5 * RG_C) * jax.nn.softplus(-lam_ref[d:d + 1, :])
        hbr = 0.5 * br_ref[d:d + 1, :]
        hbi = 0.5 * bi_ref[d:d + 1, :]
        for off, n in ((0, Tc), (Tc, T)):
            uh = 0.5 * u_scr[off:off + n, :]
            ub = uh.astype(BF16)
            t_r = jnp.tanh(jnp.dot(ub, wr_ref[d], preferred_element_type=F32) + hbr)
            t_i = jnp.tanh(jnp.dot(ub, wi_ref[d], preferred_element_type=F32) + hbi)
            log_a = t_r * hk + hk
            a = jnp.exp(log_a)
            om = -jnp.tanh(log_a) * (1.0 + a * a)
            root = om * lax.rsqrt(jnp.maximum(om, tiny))
            a_scr[d, off:off + n, :] = a
            b_scr[d, off:off + n, :] = root * (t_i * uh + uh)

    zero = jnp.zeros((SUBLANES, C), F32)
    hf, hb = _scan_both(a_scr, b_scr, hf_scr, hb_scr, 0, Tc, zero, zero)
    _scan_both(a_scr, b_scr, hf_scr, hb_scr, Tc, T, hf, hb)

    ol_ref[...] = (jax.nn.gelu(yl_ref[...].astype(F32)) * (hf_scr[Tc:Tc + T, :] + hb_scr[Tc:Tc + T, :])).astype(BF16)
    if need_ctx:
        oc_ref[...] = (jax.nn.gelu(yc_ref[...].astype(F32)) * (hf_scr[0:Tc, :] + hb_scr[0:Tc, :])).astype(BF16)


def _rnn(zl, zc, col_x, col_y, col_xc, col_yc, cw, cb, wr_bd, br, wi_bd, bi, lam, l, need_ctx, tc):
    B, T, _ = zl.shape
    Tc = zc.shape[1]
    K, C = cw.shape[1:]
    assert C % tc == 0 and T % SUBLANES == 0 and Tc % SUBLANES == 0
    assert all(o % tc == 0 for o in (col_x, col_y, col_xc)) and (col_yc is None or col_yc % tc == 0)
    nt = C // tc
    bx, by, bxc = col_x // tc, col_y // tc, col_xc // tc
    in_specs = [pl.BlockSpec((None, T, tc), lambda b, c: (b, 0, bx + c)),
                pl.BlockSpec((None, Tc, tc), lambda b, c: (b, 0, bxc + c)),
                pl.BlockSpec((None, T, tc), lambda b, c: (b, 0, by + c))]
    args = [zl, zc, zl]
    if need_ctx:
        byc = col_yc // tc
        in_specs.append(pl.BlockSpec((None, Tc, tc), lambda b, c: (b, 0, byc + c)))
        args.append(zc)
    in_specs += [pl.BlockSpec((None, K, tc), lambda b, c: (l, 0, c)),
                 pl.BlockSpec((None, 1, tc), lambda b, c: (l, 0, c)),
                 pl.BlockSpec((None, 2, None, tc, tc), lambda b, c: (l, 0, c, 0, 0)),
                 pl.BlockSpec((None, 2, tc), lambda b, c: (l, 0, c)),
                 pl.BlockSpec((None, 2, None, tc, tc), lambda b, c: (l, 0, c, 0, 0)),
                 pl.BlockSpec((None, 2, tc), lambda b, c: (l, 0, c)),
                 pl.BlockSpec((None, 2, tc), lambda b, c: (l, 0, c))]
    args += [cw, cb.reshape(cb.shape[0], 1, C), wr_bd, br, wi_bd, bi, lam]
    out_specs = [pl.BlockSpec((None, T, tc), lambda b, c: (b, 0, c))]
    out_shape = [jax.ShapeDtypeStruct((B, T, C), BF16)]
    if need_ctx:
        out_specs.append(pl.BlockSpec((None, Tc, tc), lambda b, c: (b, 0, c)))
        out_shape.append(jax.ShapeDtypeStruct((B, Tc, C), BF16))
    Tt = T + Tc
    outs = pl.pallas_call(
        functools.partial(_rnn_kernel, T=T, Tc=Tc, need_ctx=need_ctx),
        grid=(B, nt),
        in_specs=in_specs,
        out_specs=out_specs,
        out_shape=out_shape,
        scratch_shapes=[pltpu.VMEM((T + 2 * SUBLANES, tc), F32), pltpu.VMEM((Tc + 2 * SUBLANES, tc), F32),
                        pltpu.VMEM((Tt, tc), F32), pltpu.VMEM((2, Tt, tc), F32), pltpu.VMEM((2, Tt, tc), F32),
                        pltpu.VMEM((Tt, tc), F32), pltpu.VMEM((Tt, tc), F32)],
        compiler_params=_params(("parallel", "parallel")),
        name="rglru",
    )(*args)
    return (outs[0], outs[1]) if need_ctx else (outs[0], None)


def _conv_kernel(cur_ref, prev_ref, next_ref, w_ref, b_ref, lg_ref, lb_ref, o_ref, xp, sh_scr, cv_scr, wb_scr,
                 *, halo, rb, lt):
    i = pl.program_id(1)
    tt, C = o_ref.shape
    K = w_ref.shape[0]
    half = (K - 1) // 2

    def glu(ref):
        g = ref[...].astype(F32)
        return g[:, :C] * _sigmoid(g[:, C:])

    xp[0:halo, :] = jnp.where(i > 0, glu(prev_ref), 0.0)
    xp[halo:halo + tt, :] = glu(cur_ref)
    xp[halo + tt:2 * halo + tt, :] = jnp.where(i < pl.num_programs(1) - 1, glu(next_ref), 0.0)

    n_sh = tt + 2 * halo - SUBLANES
    for r in range(SUBLANES):
        sh_scr[r] = xp[pl.ds(r, n_sh), :]

    for k in range(K):
        wb_scr[k] = jnp.broadcast_to(w_ref[k:k + 1, :], (SUBLANES, C))

    groups = rb // SUBLANES

    def blk(bi, carry):
        for li in range(C // lt):
            ls = slice(li * lt, (li + 1) * lt)
            bias = jnp.broadcast_to(b_ref[:, ls], (SUBLANES, lt))
            accs = [bias] * groups
            for k in range(K):
                q, r = divmod(halo - half + k, SUBLANES)
                wk = wb_scr[k, :, ls]
                for g in range(groups):
                    base = pl.multiple_of(bi * rb + (q + g) * SUBLANES, SUBLANES)
                    accs[g] = accs[g] + wk * sh_scr[r, pl.ds(base, SUBLANES), ls]
            for g in range(groups):
                cv_scr[pl.ds(pl.multiple_of(bi * rb + g * SUBLANES, SUBLANES), SUBLANES), ls] = accs[g]
        return carry

    lax.fori_loop(0, tt // rb, blk, 0)

    u = cv_scr[...]
    mu = jnp.mean(u, axis=-1, keepdims=True)
    uc = u - mu
    var = jnp.mean(uc * uc, axis=-1, keepdims=True)
    y = uc * lax.rsqrt(var + EPS) * lg_ref[...] + lb_ref[...]
    o_ref[...] = (y * _sigmoid(y)).astype(BF16)


def _conformer(z, col_g, w, b, lg, lb, l, tt):
    B, T, _ = z.shape
    K, C = w.shape[1:]
    halo = 16
    assert (K - 1) // 2 <= halo and T % tt == 0 and tt % halo == 0 and col_g % (2 * C) == 0
    bg = col_g // (2 * C)
    th = tt // halo
    nh = T // halo
    rb = min(32, tt)
    lt = min(256, C)
    vec = lambda v: v.reshape(v.shape[0], 1, C)
    vspec = pl.BlockSpec((None, 1, C), lambda b_, i: (l, 0, 0))
    return pl.pallas_call(
        functools.partial(_conv_kernel, halo=halo, rb=rb, lt=lt),
        grid=(B, T // tt),
        in_specs=[pl.BlockSpec((None, tt, 2 * C), lambda b_, i: (b_, i, bg)),
                  pl.BlockSpec((None, halo, 2 * C), lambda b_, i: (b_, jnp.maximum(i * th - 1, 0), bg)),
                  pl.BlockSpec((None, halo, 2 * C), lambda b_, i: (b_, jnp.minimum((i + 1) * th, nh - 1), bg)),
                  pl.BlockSpec((None, K, C), lambda b_, i: (l, 0, 0)),
                  vspec, vspec, vspec],
        out_specs=pl.BlockSpec((None, tt, C), lambda b_, i: (b_, i, 0)),
        out_shape=jax.ShapeDtypeStruct((B, T, C), BF16),
        scratch_shapes=[pltpu.VMEM((tt + 2 * halo, C), F32),
                        pltpu.VMEM((SUBLANES, tt + 2 * halo - SUBLANES, C), F32),
                        pltpu.VMEM((tt, C), F32),
                        pltpu.VMEM((K, SUBLANES, C), F32)],
        compiler_params=_params(("parallel", "parallel")),
        name="conformer_conv",
    )(z, z, z, w, vec(b), vec(lg), vec(lb))


def _rope(x, cs, sn, quarter):
    lane = lax.broadcasted_iota(jnp.int32, x.shape, 1)
    partner = jnp.where(lane % (2 * quarter) < quarter,
                        pltpu.roll(x, x.shape[1] - quarter, 1), pltpu.roll(x, quarter, 1))
    return x * cs + partner * sn


def _attn_kernel(*refs, n_src, rope, lam_init, scale, hd, quarter, rs):
    it = iter(refs)
    q_ref = next(it)
    kv = [(next(it), next(it)) for _ in range(n_src)]
    if rope:
        cq_ref, sq_ref, ck_ref, sk_ref = next(it), next(it), next(it), next(it)
    dl_ref, g_ref, o_ref, k_all, v_ext = next(it), next(it), next(it), next(it), next(it)
    i = pl.program_id(2)
    tq, W = q_ref.shape
    hp = W // hd

    @pl.when(i == 0)
    def _():
        off = 0
        for s in range(n_src):
            k_ref, v_ref = kv[s]
            n = k_ref.shape[0]
            if rope and s == 0:
                k_all[off:off + n, :] = _rope(k_ref[...].astype(F32), ck_ref[...], sk_ref[...], quarter).astype(BF16)
            else:
                k_all[off:off + n, :] = k_ref[...]
            for h in range(hp):
                v_ext[h, off:off + n, 0:hd] = v_ref[:, h * hd:(h + 1) * hd]
                v_ext[h, off:off + n, hd:2 * hd] = jnp.ones((n, hd), BF16)
            off += n

    q = q_ref[...].astype(F32)
    if rope:
        q = _rope(q, cq_ref[...], sq_ref[...], quarter)
    qb = (q * (scale * math.log2(math.e))).astype(BF16)
    keys = k_all[...]

    lq = dl_ref[...]
    lam = (jnp.exp(jnp.sum(lq[0:1] * lq[1:2], axis=-1, keepdims=True))
           - jnp.exp(jnp.sum(lq[2:3] * lq[3:4], axis=-1, keepdims=True)) + lam_init)

    lane = lax.broadcasted_iota(jnp.int32, (rs, W), 1)
    half = hd // 2

    def scores(r, ch):
        lo = ch * half
        qr = qb[r * rs:(r + 1) * rs, :]
        qc = jnp.where((lane >= lo) & (lane < lo + half), qr, jnp.zeros_like(qr))
        return lax.dot_general(qc, keys, (((1,), (1,)), ((), ())), preferred_element_type=F32)

    def weights(s):
        return jnp.exp2(s - jnp.max(s, axis=-1, keepdims=True)).astype(BF16)

    def attend(ch, e):
        ne = jnp.dot(e, v_ext[ch // 2], preferred_element_type=F32)
        return ne[:, 0:hd] / ne[:, hd:2 * hd]

    chains = [(r, ch) for r in range(tq // rs) for ch in range(2 * hp)]
    s_cur = scores(*chains[0])
    e_prev = None
    comps = []
    for n in range(len(chains)):
        s_next = scores(*chains[n + 1]) if n + 1 < len(chains) else None
        e_cur = weights(s_cur)
        if e_prev is not None:
            comps.append(attend(chains[n - 1][1], e_prev))
        e_prev, s_cur = e_cur, s_next
    comps.append(attend(chains[-1][1], e_prev))

    for n in range(0, len(chains), 2):
        r, h = chains[n][0], chains[n][1] // 2
        o = comps[n] - lam * comps[n + 1]
        y = o * lax.rsqrt(jnp.mean(o * o, axis=-1, keepdims=True) + EPS) * g_ref[...] * (1.0 - lam_init)
        o_ref[r * rs:(r + 1) * rs, h * hd:(h + 1) * hd] = y.astype(BF16)


def _attention(zq, srcs, col_q, col_ks, col_vs, tabs, da_lam, subln_g, l, lam_init, n_heads, tq):
    B, Tq, _ = zq.shape
    hd = subln_g.shape[1]
    W = min(MXU_WIDTH, n_heads * hd)
    hp = W // hd
    assert hd == LANES and n_heads % hp == 0 and col_q % W == 0 and Tq % tq == 0
    assert all(ck % W == 0 for ck in col_ks) and all(cv % W == 0 for cv in col_vs)
    bq = col_q // W
    rope = tabs is not None
    in_specs = [pl.BlockSpec((None, tq, W), lambda b, h, i: (b, i, bq + h))]
    args = [zq]
    Tk_all = 0
    for zs, ck, cv in zip(srcs, col_ks, col_vs):
        Tk = zs.shape[1]
        Tk_all += Tk
        bk, bv = ck // W, cv // W
        in_specs += [pl.BlockSpec((None, Tk, W), lambda b, h, i, bk=bk: (b, 0, bk + h)),
                     pl.BlockSpec((None, Tk, W), lambda b, h, i, bv=bv: (b, 0, bv + h))]
        args += [zs, zs]
    if rope:
        cs, sn = tabs
        in_specs += [pl.BlockSpec((tq, W), lambda b, h, i: (i, 0)), pl.BlockSpec((tq, W), lambda b, h, i: (i, 0)),
                     pl.BlockSpec((Tq, W), lambda b, h, i: (0, 0)), pl.BlockSpec((Tq, W), lambda b, h, i: (0, 0))]
        args += [cs, sn, cs, sn]
    d_qk = da_lam.shape[2]
    in_specs += [pl.BlockSpec((None,) + da_lam.shape[1:], lambda b, h, i: (l, 0, 0)),
                 pl.BlockSpec((None, 1, hd), lambda b, h, i: (l, 0, 0))]
    args += [da_lam, subln_g.reshape(subln_g.shape[0], 1, hd)]
    return pl.pallas_call(
        functools.partial(_attn_kernel, n_src=len(srcs), rope=rope, lam_init=lam_init, scale=d_qk ** -0.5,
                          hd=hd, quarter=d_qk // 4, rs=min(tq, 256)),
        grid=(B, n_heads // hp, Tq // tq),
        in_specs=in_specs,
        out_specs=pl.BlockSpec((None, tq, W), lambda b, h, i: (b, i, h)),
        out_shape=jax.ShapeDtypeStruct((B, Tq, n_heads * hd), BF16),
        scratch_shapes=[pltpu.VMEM((Tk_all, W), BF16), pltpu.VMEM((hp, Tk_all, 2 * hd), BF16)],
        compiler_params=_params(("parallel", "parallel", "arbitrary")),
        name="diff_attention",
    )(*args)


def _merge_kernel(*refs, nsplit):
    r_ref, c_ref, a_ref = refs[0:3]
    z_refs = refs[3:3 + 3 * nsplit]
    x_ref, gt_ref, w0_ref, w1_ref, w2_ref, wo_ref, o_ref = refs[3 + 3 * nsplit:]
    gw = z_refs[0].shape[1]
    ins = (r_ref[...], c_ref[...], a_ref[...])
    ws = (w0_ref, w1_ref, w2_ref)
    out = None
    for p in range(nsplit):
        cols = slice(p * gw, (p + 1) * gw)
        m = None
        for j in range(3):
            gate = _sigmoid(z_refs[j * nsplit + p][...].astype(F32))
            t = gate * jnp.dot(ins[j], ws[j][:, cols], preferred_element_type=F32)
            m = t if m is None else m + t
        t = jnp.dot(m.astype(BF16), wo_ref[cols, :], preferred_element_type=F32)
        out = t if out is None else out + t
    o_ref[...] = x_ref[...] + gt_ref[...] * out


def _merge(rnn_g, conv_g, att_g, z, col_z, s, mods, row_fn, w0, w1, w2, wo, l, tm):
    M, D = s.shape
    gw = math.gcd(col_z, D)
    nsplit = D // gw
    bz = col_z // gw
    assert M % tm == 0 and gw % LANES == 0

    def act_spec(a):
        return pl.BlockSpec((tm, a.shape[1]), lambda i: (i, 0))

    def w_spec(w):
        return pl.BlockSpec((None,) + w.shape[1:], lambda i: (l, 0, 0), pipeline_mode=pl.Buffered(1))

    z_specs = [pl.BlockSpec((tm, gw), lambda i, blk=bz + j * nsplit + p: (i, blk))
               for j in range(3) for p in range(nsplit)]
    return pl.pallas_call(
        functools.partial(_merge_kernel, nsplit=nsplit),
        grid=(M // tm,),
        in_specs=[act_spec(rnn_g), act_spec(conv_g), act_spec(att_g)] + z_specs + [
            pl.BlockSpec((tm, D), lambda i: (i, 0)),
            _mod_spec(row_fn, 5, D),
            w_spec(w0), w_spec(w1), w_spec(w2), w_spec(wo)],
        out_specs=pl.BlockSpec((tm, D), lambda i: (i, 0)),
        out_shape=jax.ShapeDtypeStruct((M, D), F32),
        compiler_params=_params(("parallel",)),
        name="merge_out_proj",
    )(rnn_g, conv_g, att_g, *([z] * (3 * nsplit)), s, mods, w0, w1, w2, wo)


def _rope_tables(T, d_qk, width):
    n = d_qk // 4
    rows = T // GRID_W
    row = jnp.repeat(jnp.arange(rows, dtype=jnp.int32), GRID_W).astype(F32)
    col = jnp.tile(jnp.arange(GRID_W, dtype=jnp.int32), rows).astype(F32)
    inv = ROPE_BASE ** (-jnp.arange(n, dtype=F32) * 2.0 / (2 * n))
    ar, ac = row[:, None] * inv, col[:, None] * inv
    cs = jnp.concatenate([jnp.cos(ar), jnp.cos(ar), jnp.cos(ac), jnp.cos(ac)], axis=1)
    sn = jnp.concatenate([-jnp.sin(ar), jnp.sin(ar), -jnp.sin(ac), jnp.sin(ac)], axis=1)
    return jnp.tile(cs, (1, width // d_qk)), jnp.tile(sn, (1, width // d_qk))


def _block_diag(w, tc):
    depth, two, nb, bw, _ = w.shape
    per = tc // bw
    w6 = w.reshape(depth, two, nb // per, per, bw, bw)
    eye = jnp.eye(per, dtype=w.dtype)
    bd = jnp.einsum('ldtpij,pq->ldtpiqj', w6, eye)
    return bd.reshape(depth, two, nb // per, tc, tc).astype(BF16)


def kernel(x, c, ctx, c_ctx, ada_w, ada_b, norm_g, ffn_w_gate, ffn_w_up, ffn_w_down, w_in, rnn_conv_w, rnn_conv_b, rg_w_r, rg_b_r, rg_w_i, rg_b_i, rg_lam, rnn_w_out, cv_dw_w, cv_dw_b, cv_ln_g, cv_ln_b, cv_w_out, da_lam, da_subln_g, da_w_o, w_out, final_g):
    B, T, D = x.shape
    Tc = ctx.shape[1]
    depth = ada_w.shape[0]
    d_rnn = rnn_conv_w.shape[2]
    d_conv = cv_dw_w.shape[2]
    d_qk = da_lam.shape[2]
    d_v = da_subln_g.shape[1]
    d_att = da_w_o.shape[1]
    n_heads = d_att // d_v
    d_q = n_heads * 2 * d_qk
    splits = (d_rnn, d_rnn, 2 * d_conv, d_q, d_q, d_att, 3 * D)
    assert sum(splits) == w_in.shape[2] and 2 * d_qk == d_v
    offs = [0]
    for w_ in splits[:-1]:
        offs.append(offs[-1] + w_)
    col_x, col_y, col_g, col_q, col_k, col_v, col_z = offs

    tl = _tiles(T, T)
    tcx = _tiles(Tc, B * Tc)
    tc_rnn = min(tl["tc_rnn"], d_rnn)
    att_w = min(MXU_WIDTH, n_heads * d_v)
    tn_in = _col_tile(w_in.shape[2])

    R = -(-(B + 1) // SUBLANES) * SUBLANES
    cc = jnp.zeros((R, D), F32).at[:B].set(c).at[B].set(c_ctx)
    mods_all = _ada(cc, ada_w, ada_b).reshape(depth, R * N_MOD, 1, D)

    tabs = _rope_tables(T, d_qk, att_w)

    wg, wu, wd = ffn_w_gate.astype(BF16), ffn_w_up.astype(BF16), ffn_w_down.astype(BF16)
    w_in_b = w_in.astype(BF16)
    wr_bd = _block_diag(rg_w_r, tc_rnn)
    wi_bd = _block_diag(rg_w_i, tc_rnn)
    w_rnn, w_cv, w_da, w_o = (rnn_w_out.astype(BF16), cv_w_out.astype(BF16), da_w_o.astype(BF16),
                              w_out.astype(BF16))

    tn_c = math.gcd(math.gcd(d_rnn, col_k), math.gcd(d_q, d_att))
    ctx_blocks = tuple(range(col_x // tn_c, (col_x + d_rnn) // tn_c)) + \
        tuple(range(col_k // tn_c, (col_v + d_att) // tn_c))

    def lat_row(tm):
        per = T // tm
        return lambda i: i // per

    def ctx_row(tm):
        return lambda i: B

    xl = x.reshape(B * T, D)
    xc = ctx.reshape(B * Tc, D)
    for l in range(depth):
        need_ctx = l < depth - 1
        last = l == depth - 1
        lam_init = 0.8 - 0.6 * math.exp(-0.3 * l)
        mods = mods_all[l]

        xl = _ffn(xl, mods, lat_row(tl["tm_ffn"]), 0, norm_g[l, 0], wg, wu, wd, l, 0, tl["tm_ffn"], tl["tf"])
        xc = _ffn(xc, mods, ctx_row(tcx["tm_ffn"]), 0, norm_g[l, 0], wg, wu, wd, l, 0, tcx["tm_ffn"], tcx["tf"])

        zl = _in_proj(xl, mods, lat_row(tl["tm_in"]), norm_g[l, 1], w_in_b, l, tl["tm_in"], tn_in)
        if need_ctx:
            zc = _in_proj(xc, mods, ctx_row(tcx["tm_in"]), norm_g[l, 1], w_in_b, l, tcx["tm_in"], tn_in)
            cxc, cyc, ckc, cvc = col_x, col_y, col_k, col_v
        else:
            zc = _in_proj(xc, mods, ctx_row(tcx["tm_in"]), norm_g[l, 1], w_in_b, l, tcx["tm_in"], tn_c,
                          col_blocks=ctx_blocks)
            cxc, cyc, ckc, cvc = 0, None, d_rnn, d_rnn + d_q
        zl3 = zl.reshape(B, T, -1)
        zc3 = zc.reshape(B, Tc, -1)

        rnn_l, rnn_c = _rnn(zl3, zc3, col_x, col_y, cxc, cyc, rnn_conv_w, rnn_conv_b, wr_bd, rg_b_r, wi_bd,
                            rg_b_i, rg_lam, l, need_ctx, tc_rnn)
        conv_l = _conformer(zl3, col_g, cv_dw_w, cv_dw_b, cv_ln_g, cv_ln_b, l, tl["tt_conv"])
        att_l = _attention(zl3, [zl3, zc3], col_q, [col_k, ckc], [col_v, cvc], tabs, da_lam, da_subln_g, l,
                           lam_init, n_heads, tl["tq"])
        xl = _merge(rnn_l.reshape(B * T, -1), conv_l.reshape(B * T, -1), att_l.reshape(B * T, -1), zl, col_z,
                    xl, mods, lat_row(tl["tm_merge"]), w_rnn, w_cv, w_da, w_o, l, tl["tm_merge"])
        xl = _ffn(xl, mods, lat_row(tl["tm_ffn"]), 2, norm_g[l, 2], wg, wu, wd, l, 1, tl["tm_ffn"], tl["tf"],
                  final_g=final_g if last else None)
        if need_ctx:
            conv_c = _conformer(zc3, col_g, cv_dw_w, cv_dw_b, cv_ln_g, cv_ln_b, l, tcx["tt_conv"])
            att_c = _attention(zc3, [zc3], col_q, [col_k], [col_v], None, da_lam, da_subln_g, l,
                               lam_init, n_heads, tcx["tq"])
            xc = _merge(rnn_c.reshape(B * Tc, -1), conv_c.reshape(B * Tc, -1), att_c.reshape(B * Tc, -1), zc, col_z,
                        xc, mods, ctx_row(tcx["tm_merge"]), w_rnn, w_cv, w_da, w_o, l, tcx["tm_merge"])
            xc = _ffn(xc, mods, ctx_row(tcx["tm_ffn"]), 2, norm_g[l, 2], wg, wu, wd, l, 1,
                      tcx["tm_ffn"], tcx["tf"])
    return xl.reshape(B, T, D)
```

```python
import functools
import math

import jax
import jax.numpy as jnp
from jax import lax
from jax.experimental import pallas as pl
from jax.experimental.pallas import tpu as pltpu

F32 = jnp.float32
BF16 = jnp.bfloat16

EPS = 1e-6
GRID_W = 64
ROPE_BASE = 10000.0
RG_C = 8.0
MACARON_W = 0.5
N_MOD = 9

LANES = 128
SUBLANES = 8
MXU_WIDTH = 256
VMEM_LIMIT = 56 * 1024 * 1024


def _tiles(T, rows):
    return dict(
        tm_ffn=min(512, rows), tf=512, tm_in=min(1024, rows), tm_merge=min(256, rows),
        tt_conv=min(256, T), tq=min(1024, T), tc_rnn=256)


def _params(sem):
    return pltpu.CompilerParams(dimension_semantics=sem, vmem_limit_bytes=VMEM_LIMIT)


def _mod_spec(row_fn, k, D):
    return pl.BlockSpec((None, 1, D), lambda i, *_: (row_fn(i) * N_MOD + k, 0, 0))


def _rms_mod(x, ng, sc, sh):
    ms = jnp.mean(x * x, axis=-1, keepdims=True)
    return (x * lax.rsqrt(ms + EPS)) * (ng * (1.0 + sc)) + sh


def _sigmoid(x):
    return 0.5 * jnp.tanh(0.5 * x) + 0.5


def _ada_kernel(c_ref, w_ref, b_ref, o_ref):
    c = c_ref[...]
    a = (c * _sigmoid(c)).astype(BF16)
    o_ref[...] = jnp.dot(a, w_ref[...].astype(BF16), preferred_element_type=F32) + b_ref[...]


def _ada(cc, ada_w, ada_b):
    depth, D, N = ada_w.shape
    R = cc.shape[0]
    tn = min(1024, D)
    assert N % tn == 0
    return pl.pallas_call(
        _ada_kernel,
        grid=(depth, N // tn),
        in_specs=[pl.BlockSpec((R, D), lambda l, j: (0, 0)),
                  pl.BlockSpec((None, D, tn), lambda l, j: (l, 0, j)),
                  pl.BlockSpec((None, 1, tn), lambda l, j: (l, 0, j))],
        out_specs=pl.BlockSpec((None, R, tn), lambda l, j: (l, 0, j)),
        out_shape=jax.ShapeDtypeStruct((depth, R, N), F32),
        compiler_params=_params(("arbitrary", "arbitrary")),
        name="ada_mods",
    )(cc, ada_w, ada_b.reshape(depth, 1, N))


def _ffn_kernel(*refs, final):
    if final:
        s_ref, sh_ref, sc_ref, gt_ref, ng_ref, wg_ref, wu_ref, wd_ref, fg_ref, o_ref, h_scr = refs
    else:
        s_ref, sh_ref, sc_ref, gt_ref, ng_ref, wg_ref, wu_ref, wd_ref, o_ref, h_scr = refs
    j = pl.program_id(1)

    @pl.when(j == 0)
    def _():
        h_scr[...] = _rms_mod(s_ref[...], ng_ref[...], sc_ref[...], sh_ref[...]).astype(BF16)
        o_ref[...] = jnp.zeros_like(o_ref)

    h = h_scr[...]
    g = jnp.dot(h, wg_ref[...], preferred_element_type=F32)
    u = jnp.dot(h, wu_ref[...], preferred_element_type=F32)
    a = (g * _sigmoid(g) * u).astype(BF16)
    o_ref[...] += jnp.dot(a, wd_ref[...], preferred_element_type=F32)

    @pl.when(j == pl.num_programs(1) - 1)
    def _():
        y = s_ref[...] + (MACARON_W * gt_ref[...]) * o_ref[...]
        if final:
            y = y * lax.rsqrt(jnp.mean(y * y, axis=-1, keepdims=True) + EPS) * fg_ref[...]
        o_ref[...] = y


def _ffn(s, mods, row_fn, k, ng, wg, wu, wd, l, which, tm, tf, final_g=None):
    M, D = s.shape
    nf = wg.shape[2]
    assert M % tm == 0 and wg.shape[4] == tf and wd.shape[2] == nf * tf
    in_specs = [pl.BlockSpec((tm, D), lambda i, j: (i, 0)),
                _mod_spec(row_fn, 3 * k, D), _mod_spec(row_fn, 3 * k + 1, D), _mod_spec(row_fn, 3 * k + 2, D),
                pl.BlockSpec((1, D), lambda i, j: (0, 0)),
                pl.BlockSpec((None, None, None, D, tf), lambda i, j: (l, which, j, 0, 0)),
                pl.BlockSpec((None, None, None, D, tf), lambda i, j: (l, which, j, 0, 0)),
                pl.BlockSpec((None, None, tf, D), lambda i, j: (l, which, j, 0))]
    args = [s, mods, mods, mods, ng.reshape(1, D), wg, wu, wd]
    if final_g is not None:
        in_specs.append(pl.BlockSpec((1, D), lambda i, j: (0, 0)))
        args.append(final_g.reshape(1, D))
    return pl.pallas_call(
        functools.partial(_ffn_kernel, final=final_g is not None),
        grid=(M // tm, nf),
        in_specs=in_specs,
        out_specs=pl.BlockSpec((tm, D), lambda i, j: (i, 0)),
        out_shape=jax.ShapeDtypeStruct((M, D), F32),
        scratch_shapes=[pltpu.VMEM((tm, D), BF16)],
        compiler_params=_params(("parallel", "arbitrary")),
        name="ffn_half_step",
    )(*args)


def _in_kernel(*refs, prefetch):
    if prefetch:
        refs = refs[1:]
    x_ref, sh_ref, sc_ref, ng_ref, w_ref, o_ref, h_scr = refs

    @pl.when(pl.program_id(1) == 0)
    def _():
        h_scr[...] = _rms_mod(x_ref[...], ng_ref[...], sc_ref[...], sh_ref[...]).astype(BF16)

    o_ref[...] = jnp.dot(h_scr[...], w_ref[...], preferred_element_type=F32).astype(BF16)


def _in_proj(s, mods, row_fn, ng, w, l, tm, col_blocks=None):
    M, D = s.shape
    tn = w.shape[3]
    assert M % tm == 0
    nb = w.shape[1] if col_blocks is None else len(col_blocks)
    prefetch = col_blocks is not None
    if prefetch:
        w_map = lambda i, j, cb: (l, cb[j], 0, 0)
    else:
        w_map = lambda i, j: (l, j, 0, 0)
    grid_spec = pltpu.PrefetchScalarGridSpec(
        num_scalar_prefetch=1 if prefetch else 0,
        grid=(M // tm, nb),
        in_specs=[pl.BlockSpec((tm, D), lambda i, j, *_: (i, 0)),
                  _mod_spec(row_fn, 3, D), _mod_spec(row_fn, 4, D),
                  pl.BlockSpec((1, D), lambda i, j, *_: (0, 0)),
                  pl.BlockSpec((None, None, D, tn), w_map)],
        out_specs=pl.BlockSpec((tm, tn), lambda i, j, *_: (i, j)),
        scratch_shapes=[pltpu.VMEM((tm, D), BF16)])
    args = [s, mods, mods, ng.reshape(1, D), w]
    if prefetch:
        args = [jnp.asarray(col_blocks, jnp.int32)] + args
    return pl.pallas_call(
        functools.partial(_in_kernel, prefetch=prefetch),
        grid_spec=grid_spec,
        out_shape=jax.ShapeDtypeStruct((M, nb * tn), BF16),
        compiler_params=_params(("parallel", "arbitrary")),
        name="in_proj",
    )(*args)


def _prefix8(A, Bv, row, reverse):
    for s in (1, 2, 4):
        if reverse:
            valid = row < SUBLANES - s
            sh = SUBLANES - s
        else:
            valid = row >= s
            sh = s
        As = jnp.where(valid, pltpu.roll(A, sh, 0), 1.0)
        Bs = jnp.where(valid, pltpu.roll(Bv, sh, 0), 0.0)
        Bv = A * Bs + Bv
        A = A * As
    return A, Bv


def _scan_both(a_scr, b_scr, hf_scr, hb_scr, start, n, hf, hb):
    C = hf_scr.shape[1]
    row = lax.broadcasted_iota(jnp.int32, (SUBLANES, C), 0)
    nblk = n // SUBLANES

    def body(i, carry):
        hf, hb = carry
        bf = pl.multiple_of(start + i * SUBLANES, SUBLANES)
        bb = pl.multiple_of(start + (nblk - 1 - i) * SUBLANES, SUBLANES)
        Af, Bf = _prefix8(a_scr[0, pl.ds(bf, SUBLANES), :], b_scr[0, pl.ds(bf, SUBLANES), :], row, False)
        Ab, Bb = _prefix8(a_scr[1, pl.ds(bb, SUBLANES), :], b_scr[1, pl.ds(bb, SUBLANES), :], row, True)
        Hf = Bf + Af * hf
        Hb = Bb + Ab * hb
        hf_scr[pl.ds(bf, SUBLANES), :] = Hf
        hb_scr[pl.ds(bb, SUBLANES), :] = Hb
        return (jnp.broadcast_to(Hf[SUBLANES - 1:SUBLANES, :], (SUBLANES, C)),
                jnp.broadcast_to(Hb[0:1, :], (SUBLANES, C)))

    return lax.fori_loop(0, nblk, body, (hf, hb), unroll=8)


def _rnn_kernel(*refs, T, Tc, need_ctx):
    if need_ctx:
        (xl_ref, xc_ref, yl_ref, yc_ref, cw_ref, cb_ref, wr_ref, br_ref, wi_ref, bi_ref, lam_ref,
         ol_ref, oc_ref, xp_l, xp_c, u_scr, a_scr, b_scr, hf_scr, hb_scr) = refs
    else:
        (xl_ref, xc_ref, yl_ref, cw_ref, cb_ref, wr_ref, br_ref, wi_ref, bi_ref, lam_ref,
         ol_ref, xp_l, xp_c, u_scr, a_scr, b_scr, hf_scr, hb_scr) = refs
    C = u_scr.shape[1]
    K = cw_ref.shape[0]
    P = SUBLANES

    def conv(x_ref, xp, off, n):
        xp[0:P, :] = jnp.zeros((P, C), F32)
        xp[P + n:2 * P + n, :] = jnp.zeros((P, C), F32)
        xp[P:P + n, :] = x_ref[...].astype(F32)
        acc = jnp.broadcast_to(cb_ref[...], (n, C))
        lo = (K - 1) // 2
        for k in range(K):
            acc = acc + cw_ref[k:k + 1, :] * xp[pl.ds(P - lo + k, n), :]
        u_scr[off:off + n, :] = acc

    conv(xc_ref, xp_c, 0, Tc)
    conv(xl_ref, xp_l, Tc, T)

    tiny = jnp.finfo(F32).tiny
    for d in range(2):
        hk = (-0.5 * RG_C) * jax.nn.softplus(-lam_ref[d:d + 1, :])
        hbr = 0.5 * br_ref[d:d + 1, :]
        hbi = 0.5 * bi_ref[d:d + 1, :]
        for off, n in ((0, Tc), (Tc, T)):
            uh = 0.5 * u_scr[off:off + n, :]
            ub = uh.astype(BF16)
            t_r = jnp.tanh(jnp.dot(ub, wr_ref[d], preferred_element_type=F32) + hbr)
            t_i = jnp.tanh(jnp.dot(ub, wi_ref[d], preferred_element_type=F32) + hbi)
            log_a = t_r * hk + hk
            a = jnp.exp(log_a)
            om = -jnp.tanh(log_a) * (1.0 + a * a)
            root = om * lax.rsqrt(jnp.maximum(om, tiny))
            a_scr[d, off:off + n, :] = a
            b_scr[d, off:off + n, :] = root * (t_i * uh + uh)

    zero = jnp.zeros((SUBLANES, C), F32)
    hf, hb = _scan_both(a_scr, b_scr, hf_scr, hb_scr, 0, Tc, zero, zero)
    _scan_both(a_scr, b_scr, hf_scr, hb_scr, Tc, T, hf, hb)

    ol_ref[...] = (jax.nn.gelu(yl_ref[...].astype(F32)) * (hf_scr[Tc:Tc + T, :] + hb_scr[Tc:Tc + T, :])).astype(BF16)
    if need_ctx:
        oc_ref[...] = (jax.nn.gelu(yc_ref[...].astype(F32)) * (hf_scr[0:Tc, :] + hb_scr[0:Tc, :])).astype(BF16)


def _rnn(zl, zc, col_x, col_y, col_xc, col_yc, cw, cb, wr_bd, br, wi_bd, bi, lam, l, need_ctx, tc):
    B, T, _ = zl.shape
    Tc = zc.shape[1]
    K, C = cw.shape[1:]
    assert C % tc == 0 and T % SUBLANES == 0 and Tc % SUBLANES == 0
    assert all(o % tc == 0 for o in (col_x, col_y, col_xc)) and (col_yc is None or col_yc % tc == 0)
    nt = C // tc
    bx, by, bxc = col_x // tc, col_y // tc, col_xc // tc
    in_specs = [pl.BlockSpec((None, T, tc), lambda b, c: (b, 0, bx + c)),
                pl.BlockSpec((None, Tc, tc), lambda b, c: (b, 0, bxc + c)),
                pl.BlockSpec((None, T, tc), lambda b, c: (b, 0, by + c))]
    args = [zl, zc, zl]
    if need_ctx:
        byc = col_yc // tc
        in_specs.append(pl.BlockSpec((None, Tc, tc), lambda b, c: (b, 0, byc + c)))
        args.append(zc)
    in_specs += [pl.BlockSpec((None, K, tc), lambda b, c: (l, 0, c)),
                 pl.BlockSpec((None, 1, tc), lambda b, c: (l, 0, c)),
                 pl.BlockSpec((None, 2, None, tc, tc), lambda b, c: (l, 0, c, 0, 0)),
                 pl.BlockSpec((None, 2, tc), lambda b, c: (l, 0, c)),
                 pl.BlockSpec((None, 2, None, tc, tc), lambda b, c: (l, 0, c, 0, 0)),
                 pl.BlockSpec((None, 2, tc), lambda b, c: (l, 0, c)),
                 pl.BlockSpec((None, 2, tc), lambda b, c: (l, 0, c))]
    args += [cw, cb.reshape(cb.shape[0], 1, C), wr_bd, br, wi_bd, bi, lam]
    out_specs = [pl.BlockSpec((None, T, tc), lambda b, c: (b, 0, c))]
    out_shape = [jax.ShapeDtypeStruct((B, T, C), BF16)]
    if need_ctx:
        out_specs.append(pl.BlockSpec((None, Tc, tc), lambda b, c: (b, 0, c)))
        out_shape.append(jax.ShapeDtypeStruct((B, Tc, C), BF16))
    Tt = T + Tc
    outs = pl.pallas_call(
        functools.partial(_rnn_kernel, T=T, Tc=Tc, need_ctx=need_ctx),
        grid=(B, nt),
        in_specs=in_specs,
        out_specs=out_specs,
        out_shape=out_shape,
        scratch_shapes=[pltpu.VMEM((T + 2 * SUBLANES, tc), F32), pltpu.VMEM((Tc + 2 * SUBLANES, tc), F32),
                        pltpu.VMEM((Tt, tc), F32), pltpu.VMEM((2, Tt, tc), F32), pltpu.VMEM((2, Tt, tc), F32),
                        pltpu.VMEM((Tt, tc), F32), pltpu.VMEM((Tt, tc), F32)],
        compiler_params=_params(("parallel", "parallel")),
        name="rglru",
    )(*args)
    return (outs[0], outs[1]) if need_ctx else (outs[0], None)


def _conv_kernel(cur_ref, prev_ref, next_ref, w_ref, b_ref, lg_ref, lb_ref, o_ref, xp, sh_scr, cv_scr, wb_scr,
                 *, halo, rb, lt):
    i = pl.program_id(1)
    tt, C = o_ref.shape
    K = w_ref.shape[0]
    half = (K - 1) // 2

    def glu(ref):
        g = ref[...].astype(F32)
        return g[:, :C] * _sigmoid(g[:, C:])

    xp[0:halo, :] = jnp.where(i > 0, glu(prev_ref), 0.0)
    xp[halo:halo + tt, :] = glu(cur_ref)
    xp[halo + tt:2 * halo + tt, :] = jnp.where(i < pl.num_programs(1) - 1, glu(next_ref), 0.0)

    n_sh = tt + 2 * halo - SUBLANES
    for r in range(SUBLANES):
        sh_scr[r] = xp[pl.ds(r, n_sh), :]

    for k in range(K):
        wb_scr[k] = jnp.broadcast_to(w_ref[k:k + 1, :], (SUBLANES, C))

    groups = rb // SUBLANES

    def blk(bi, carry):
        for li in range(C // lt):
            ls = slice(li * lt, (li + 1) * lt)
            bias = jnp.broadcast_to(b_ref[:, ls], (SUBLANES, lt))
            accs = [bias] * groups
            for k in range(K):
                q, r = divmod(halo - half + k, SUBLANES)
                wk = wb_scr[k, :, ls]
                for g in range(groups):
                    base = pl.multiple_of(bi * rb + (q + g) * SUBLANES, SUBLANES)
                    accs[g] = accs[g] + wk * sh_scr[r, pl.ds(base, SUBLANES), ls]
            for g in range(groups):
                cv_scr[pl.ds(pl.multiple_of(bi * rb + g * SUBLANES, SUBLANES), SUBLANES), ls] = accs[g]
        return carry

    lax.fori_loop(0, tt // rb, blk, 0)

    u = cv_scr[...]
    mu = jnp.mean(u, axis=-1, keepdims=True)
    uc = u - mu
    var = jnp.mean(uc * uc, axis=-1, keepdims=True)
    y = uc * lax.rsqrt(var + EPS) * lg_ref[...] + lb_ref[...]
    o_ref[...] = (y * _sigmoid(y)).astype(BF16)


def _conformer(z, col_g, w, b, lg, lb, l, tt):
    B, T, _ = z.shape
    K, C = w.shape[1:]
    halo = 16
    assert (K - 1) // 2 <= halo and T % tt == 0 and tt % halo == 0 and col_g % (2 * C) == 0
    bg = col_g // (2 * C)
    th = tt // halo
    nh = T // halo
    rb = min(32, tt)
    lt = min(256, C)
    vec = lambda v: v.reshape(v.shape[0], 1, C)
    vspec = pl.BlockSpec((None, 1, C), lambda b_, i: (l, 0, 0))
    return pl.pallas_call(
        functools.partial(_conv_kernel, halo=halo, rb=rb, lt=lt),
        grid=(B, T // tt),
        in_specs=[pl.BlockSpec((None, tt, 2 * C), lambda b_, i: (b_, i, bg)),
                  pl.BlockSpec((None, halo, 2 * C), lambda b_, i: (b_, jnp.maximum(i * th - 1, 0), bg)),
                  pl.BlockSpec((None, halo, 2 * C), lambda b_, i: (b_, jnp.minimum((i + 1) * th, nh - 1), bg)),
                  pl.BlockSpec((None, K, C), lambda b_, i: (l, 0, 0)),
                  vspec, vspec, vspec],
        out_specs=pl.BlockSpec((None, tt, C), lambda b_, i: (b_, i, 0)),
        out_shape=jax.ShapeDtypeStruct((B, T, C), BF16),
        scratch_shapes=[pltpu.VMEM((tt + 2 * halo, C), F32),
                        pltpu.VMEM((SUBLANES, tt + 2 * halo - SUBLANES, C), F32),
                        pltpu.VMEM((tt, C), F32),
                        pltpu.VMEM((K, SUBLANES, C), F32)],
        compiler_params=_params(("parallel", "parallel")),
        name="conformer_conv",
    )(z, z, z, w, vec(b), vec(lg), vec(lb))


def _rope(x, cs, sn, quarter):
    lane = lax.broadcasted_iota(jnp.int32, x.shape, 1)
    partner = jnp.where(lane % (2 * quarter) < quarter,
                        pltpu.roll(x, x.shape[1] - quarter, 1), pltpu.roll(x, quarter, 1))
    return x * cs + partner * sn


def _attn_kernel(*refs, n_src, rope, lam_init, scale, hd, quarter, rs):
    it = iter(refs)
    q_ref = next(it)
    kv = [(next(it), next(it)) for _ in range(n_src)]
    if rope:
        cq_ref, sq_ref, ck_ref, sk_ref = next(it), next(it), next(it), next(it)
    dl_ref, g_ref, o_ref, k_all, v_ext = next(it), next(it), next(it), next(it), next(it)
    i = pl.program_id(2)
    tq, W = q_ref.shape
    hp = W // hd

    @pl.when(i == 0)
    def _():
        off = 0
        for s in range(n_src):
            k_ref, v_ref = kv[s]
            n = k_ref.shape[0]
            if rope and s == 0:
                k_all[off:off + n, :] = _rope(k_ref[...].astype(F32), ck_ref[...], sk_ref[...], quarter).astype(BF16)
            else:
                k_all[off:off + n, :] = k_ref[...]
            for h in range(hp):
                v_ext[h, off:off + n, 0:hd] = v_ref[:, h * hd:(h + 1) * hd]
                v_ext[h, off:off + n, hd:2 * hd] = jnp.ones((n, hd), BF16)
            off += n

    q = q_ref[...].astype(F32)
    if rope:
        q = _rope(q, cq_ref[...], sq_ref[...], quarter)
    qb = (q * (scale * math.log2(math.e))).astype(BF16)
    keys = k_all[...]

    lq = dl_ref[...]
    lam = (jnp.exp(jnp.sum(lq[0:1] * lq[1:2], axis=-1, keepdims=True))
           - jnp.exp(jnp.sum(lq[2:3] * lq[3:4], axis=-1, keepdims=True)) + lam_init)

    lane = lax.broadcasted_iota(jnp.int32, (rs, W), 1)
    half = hd // 2

    def scores(r, ch):
        lo = ch * half
        qr = qb[r * rs:(r + 1) * rs, :]
        qc = jnp.where((lane >= lo) & (lane < lo + half), qr, jnp.zeros_like(qr))
        return lax.dot_general(qc, keys, (((1,), (1,)), ((), ())), preferred_element_type=F32)

    def weights(s):
        return jnp.exp2(s - jnp.max(s, axis=-1, keepdims=True)).astype(BF16)

    def attend(ch, e):
        ne = jnp.dot(e, v_ext[ch // 2], preferred_element_type=F32)
        return ne[:, 0:hd] / ne[:, hd:2 * hd]

    chains = [(r, ch) for r in range(tq // rs) for ch in range(2 * hp)]
    s_cur = scores(*chains[0])
    e_prev = None
    comps = []
    for n in range(len(chains)):
        s_next = scores(*chains[n + 1]) if n + 1 < len(chains) else None
        e_cur = weights(s_cur)
        if e_prev is not None:
            comps.append(attend(chains[n - 1][1], e_prev))
        e_prev, s_cur = e_cur, s_next
    comps.append(attend(chains[-1][1], e_prev))

    for n in range(0, len(chains), 2):
        r, h = chains[n][0], chains[n][1] // 2
        o = comps[n] - lam * comps[n + 1]
        y = o * lax.rsqrt(jnp.mean(o * o, axis=-1, keepdims=True) + EPS) * g_ref[...] * (1.0 - lam_init)
        o_ref[r * rs:(r + 1) * rs, h * hd:(h + 1) * hd] = y.astype(BF16)


def _attention(zq, srcs, col_q, col_ks, col_vs, tabs, da_lam, subln_g, l, lam_init, n_heads, tq):
    B, Tq, _ = zq.shape
    hd = subln_g.shape[1]
    W = min(MXU_WIDTH, n_heads * hd)
    hp = W // hd
    assert hd == LANES and n_heads % hp == 0 and col_q % W == 0 and Tq % tq == 0
    assert all(ck % W == 0 for ck in col_ks) and all(cv % W == 0 for cv in col_vs)
    bq = col_q // W
    rope = tabs is not None
    in_specs = [pl.BlockSpec((None, tq, W), lambda b, h, i: (b, i, bq + h))]
    args = [zq]
    Tk_all = 0
    for zs, ck, cv in zip(srcs, col_ks, col_vs):
        Tk = zs.shape[1]
        Tk_all += Tk
        bk, bv = ck // W, cv // W
        in_specs += [pl.BlockSpec((None, Tk, W), lambda b, h, i, bk=bk: (b, 0, bk + h)),
                     pl.BlockSpec((None, Tk, W), lambda b, h, i, bv=bv: (b, 0, bv + h))]
        args += [zs, zs]
    if rope:
        cs, sn = tabs
        in_specs += [pl.BlockSpec((tq, W), lambda b, h, i: (i, 0)), pl.BlockSpec((tq, W), lambda b, h, i: (i, 0)),
                     pl.BlockSpec((Tq, W), lambda b, h, i: (0, 0)), pl.BlockSpec((Tq, W), lambda b, h, i: (0, 0))]
        args += [cs, sn, cs, sn]
    d_qk = da_lam.shape[2]
    in_specs += [pl.BlockSpec((None,) + da_lam.shape[1:], lambda b, h, i: (l, 0, 0)),
                 pl.BlockSpec((None, 1, hd), lambda b, h, i: (l, 0, 0))]
    args += [da_lam, subln_g.reshape(subln_g.shape[0], 1, hd)]
    return pl.pallas_call(
        functools.partial(_attn_kernel, n_src=len(srcs), rope=rope, lam_init=lam_init, scale=d_qk ** -0.5,
                          hd=hd, quarter=d_qk // 4, rs=min(tq, 256)),
        grid=(B, n_heads // hp, Tq // tq),
        in_specs=in_specs,
        out_specs=pl.BlockSpec((None, tq, W), lambda b, h, i: (b, i, h)),
        out_shape=jax.ShapeDtypeStruct((B, Tq, n_heads * hd), BF16),
        scratch_shapes=[pltpu.VMEM((Tk_all, W), BF16), pltpu.VMEM((hp, Tk_all, 2 * hd), BF16)],
        compiler_params=_params(("parallel", "parallel", "arbitrary")),
        name="diff_attention",
    )(*args)


def _merge_kernel(*refs, nsplit):
    r_ref, c_ref, a_ref = refs[0:3]
    z_refs = refs[3:3 + 3 * nsplit]
    x_ref, gt_ref, w0_ref, w1_ref, w2_ref, wo_ref, o_ref = refs[3 + 3 * nsplit:]
    gw = z_refs[0].shape[1]
    ins = (r_ref[...], c_ref[...], a_ref[...])
    ws = (w0_ref, w1_ref, w2_ref)
    out = None
    for p in range(nsplit):
        cols = slice(p * gw, (p + 1) * gw)
        m = None
        for j in range(3):
            gate = _sigmoid(z_refs[j * nsplit + p][...].astype(F32))
            t = gate * jnp.dot(ins[j], ws[j][:, cols], preferred_element_type=F32)
            m = t if m is None else m + t
        t = jnp.dot(m.astype(BF16), wo_ref[cols, :], preferred_element_type=F32)
        out = t if out is None else out + t
    o_ref[...] = x_ref[...] + gt_ref[...] * out


def _merge(rnn_g, conv_g, att_g, z, col_z, s, mods, row_fn, w0, w1, w2, wo, l, tm):
    M, D = s.shape
    gw = math.gcd(col_z, D)
    nsplit = D // gw
    bz = col_z // gw
    assert M % tm == 0 and gw % LANES == 0

    def act_spec(a):
        return pl.BlockSpec((tm, a.shape[1]), lambda i: (i, 0))

    def w_spec(w):
        return pl.BlockSpec((None,) + w.shape[1:], lambda i: (l, 0, 0), pipeline_mode=pl.Buffered(1))

    z_specs = [pl.BlockSpec((tm, gw), lambda i, blk=bz + j * nsplit + p: (i, blk))
               for j in range(3) for p in range(nsplit)]
    return pl.pallas_call(
        functools.partial(_merge_kernel, nsplit=nsplit),
        grid=(M // tm,),
        in_specs=[act_spec(rnn_g), act_spec(conv_g), act_spec(att_g)] + z_specs + [
            pl.BlockSpec((tm, D), lambda i: (i, 0)),
            _mod_spec(row_fn, 5, D),
            w_spec(w0), w_spec(w1), w_spec(w2), w_spec(wo)],
        out_specs=pl.BlockSpec((tm, D), lambda i: (i, 0)),
        out_shape=jax.ShapeDtypeStruct((M, D), F32),
        compiler_params=_params(("parallel",)),
        name="merge_out_proj",
    )(rnn_g, conv_g, att_g, *([z] * (3 * nsplit)), s, mods, w0, w1, w2, wo)


def _rope_tables(T, d_qk, width):
    n = d_qk // 4
    rows = T // GRID_W
    row = jnp.repeat(jnp.arange(rows, dtype=jnp.int32), GRID_W).astype(F32)
    col = jnp.tile(jnp.arange(GRID_W, dtype=jnp.int32), rows).astype(F32)
    inv = ROPE_BASE ** (-jnp.arange(n, dtype=F32) * 2.0 / (2 * n))
    ar, ac = row[:, None] * inv, col[:, None] * inv
    cs = jnp.concatenate([jnp.cos(ar), jnp.cos(ar), jnp.cos(ac), jnp.cos(ac)], axis=1)
    sn = jnp.concatenate([-jnp.sin(ar), jnp.sin(ar), -jnp.sin(ac), jnp.sin(ac)], axis=1)
    return jnp.tile(cs, (1, width // d_qk)), jnp.tile(sn, (1, width // d_qk))


def _block_diag(w, tc):
    depth, two, nb, bw, _ = w.shape
    per = tc // bw
    w6 = w.reshape(depth, two, nb // per, per, bw, bw)
    eye = jnp.eye(per, dtype=w.dtype)
    bd = jnp.einsum('ldtpij,pq->ldtpiqj', w6, eye)
    return bd.reshape(depth, two, nb // per, tc, tc).astype(BF16)


def _col_tiles(w, t):
    lead, (k, n) = w.shape[:-2], w.shape[-2:]
    return jnp.moveaxis(w.astype(BF16).reshape(lead + (k, n // t, t)), -2, -3)


def kernel(x, c, ctx, c_ctx, ada_w, ada_b, norm_g, ffn_w_gate, ffn_w_up, ffn_w_down, w_in, rnn_conv_w, rnn_conv_b, rg_w_r, rg_b_r, rg_w_i, rg_b_i, rg_lam, rnn_w_out, cv_dw_w, cv_dw_b, cv_ln_g, cv_ln_b, cv_w_out, da_lam, da_subln_g, da_w_o, w_out, final_g):
    B, T, D = x.shape
    Tc = ctx.shape[1]
    depth = ada_w.shape[0]
    d_rnn = rnn_conv_w.shape[2]
    d_conv = cv_dw_w.shape[2]
    d_qk = da_lam.shape[2]
    d_v = da_subln_g.shape[1]
    d_att = da_w_o.shape[1]
    n_heads = d_att // d_v
    d_q = n_heads * 2 * d_qk
    splits = (d_rnn, d_rnn, 2 * d_conv, d_q, d_q, d_att, 3 * D)
    assert sum(splits) == w_in.shape[2] and 2 * d_qk == d_v
    offs = [0]
    for w_ in splits[:-1]:
        offs.append(offs[-1] + w_)
    col_x, col_y, col_g, col_q, col_k, col_v, col_z = offs

    tl = _tiles(T, T)
    tcx = _tiles(Tc, B * Tc)
    tc_rnn = min(tl["tc_rnn"], d_rnn)
    att_w = min(MXU_WIDTH, n_heads * d_v)

    R = -(-(B + 1) // SUBLANES) * SUBLANES
    cc = jnp.zeros((R, D), F32).at[:B].set(c).at[B].set(c_ctx)
    mods_all = _ada(cc, ada_w, ada_b).reshape(depth, R * N_MOD, 1, D)

    tabs = _rope_tables(T, d_qk, att_w)

    tn_in = math.gcd(math.gcd(d_rnn, col_k), math.gcd(d_q, d_att))
    ctx_blocks = tuple(range(col_x // tn_in, (col_x + d_rnn) // tn_in)) + \
        tuple(range(col_k // tn_in, (col_v + d_att) // tn_in))

    wg, wu, wd = _col_tiles(ffn_w_gate, tl["tf"]), _col_tiles(ffn_w_up, tl["tf"]), ffn_w_down.astype(BF16)
    w_in_b = _col_tiles(w_in, tn_in)
    wr_bd = _block_diag(rg_w_r, tc_rnn)
    wi_bd = _block_diag(rg_w_i, tc_rnn)
    w_rnn, w_cv, w_da, w_o = (rnn_w_out.astype(BF16), cv_w_out.astype(BF16), da_w_o.astype(BF16),
                              w_out.astype(BF16))

    def lat_row(tm):
        per = T // tm
        return lambda i: i // per

    def ctx_row(tm):
        return lambda i: B

    xl = x.reshape(B * T, D)
    xc = ctx.reshape(B * Tc, D)
    for l in range(depth):
        need_ctx = l < depth - 1
        last = l == depth - 1
        lam_init = 0.8 - 0.6 * math.exp(-0.3 * l)
        mods = mods_all[l]

        xl = _ffn(xl, mods, lat_row(tl["tm_ffn"]), 0, norm_g[l, 0], wg, wu, wd, l, 0, tl["tm_ffn"], tl["tf"])
        xc = _ffn(xc, mods, ctx_row(tcx["tm_ffn"]), 0, norm_g[l, 0], wg, wu, wd, l, 0, tcx["tm_ffn"], tcx["tf"])

        zl = _in_proj(xl, mods, lat_row(tl["tm_in"]), norm_g[l, 1], w_in_b, l, tl["tm_in"])
        if need_ctx:
            zc = _in_proj(xc, mods, ctx_row(tcx["tm_in"]), norm_g[l, 1], w_in_b, l, tcx["tm_in"])
            cxc, cyc, ckc, cvc = col_x, col_y, col_k, col_v
        else:
            zc = _in_proj(xc, mods, ctx_row(tcx["tm_in"]), norm_g[l, 1], w_in_b, l, tcx["tm_in"],
                          col_blocks=ctx_blocks)
            cxc, cyc, ckc, cvc = 0, None, d_rnn, d_rnn + d_q
        zl3 = zl.reshape(B, T, -1)
        zc3 = zc.reshape(B, Tc, -1)

        rnn_l, rnn_c = _rnn(zl3, zc3, col_x, col_y, cxc, cyc, rnn_conv_w, rnn_conv_b, wr_bd, rg_b_r, wi_bd,
                            rg_b_i, rg_lam, l, need_ctx, tc_rnn)
        conv_l = _conformer(zl3, col_g, cv_dw_w, cv_dw_b, cv_ln_g, cv_ln_b, l, tl["tt_conv"])
        att_l = _attention(zl3, [zl3, zc3], col_q, [col_k, ckc], [col_v, cvc], tabs, da_lam, da_subln_g, l,
                           lam_init, n_heads, tl["tq"])
        xl = _merge(rnn_l.reshape(B * T, -1), conv_l.reshape(B * T, -1), att_l.reshape(B * T, -1), zl, col_z,
                    xl, mods, lat_row(tl["tm_merge"]), w_rnn, w_cv, w_da, w_o, l, tl["tm_merge"])
        xl = _ffn(xl, mods, lat_row(tl["tm_ffn"]), 2, norm_g[l, 2], wg, wu, wd, l, 1, tl["tm_ffn"], tl["tf"],
                  final_g=final_g if last else None)
        if need_ctx:
            conv_c = _conformer(zc3, col_g, cv_dw_w, cv_dw_b, cv_ln_g, cv_ln_b, l, tcx["tt_conv"])
            att_c = _attention(zc3, [zc3], col_q, [col_k], [col_v], None, da_lam, da_subln_g, l,
                               lam_init, n_heads, tcx["tq"])
            xc = _merge(rnn_c.reshape(B * Tc, -1), conv_c.reshape(B * Tc, -1), att_c.reshape(B * Tc, -1), zc, col_z,
                        xc, mods, ctx_row(tcx["tm_merge"]), w_rnn, w_cv, w_da, w_o, l, tcx["tm_merge"])
            xc = _ffn(xc, mods, ctx_row(tcx["tm_ffn"]), 2, norm_g[l, 2], wg, wu, wd, l, 1,
                      tcx["tm_ffn"], tcx["tf"])
    return xl.reshape(B, T, D)
```

```python
import functools
import math

import jax
import jax.numpy as jnp
from jax import lax
from jax.experimental import pallas as pl
from jax.experimental.pallas import tpu as pltpu

F32 = jnp.float32
BF16 = jnp.bfloat16

EPS = 1e-6
GRID_W = 64
ROPE_BASE = 10000.0
RG_C = 8.0
MACARON_W = 0.5
N_MOD = 9

LANES = 128
SUBLANES = 8
MXU_WIDTH = 256
VMEM_LIMIT = 56 * 1024 * 1024


def _tiles(T, rows):
    return dict(
        tm_ffn=min(512, rows), tf=512, tm_in=min(1024, rows), tm_merge=min(256, rows),
        tt_conv=min(256, T), tq=min(1024, T), tc_rnn=256)


def _params(sem):
    return pltpu.CompilerParams(dimension_semantics=sem, vmem_limit_bytes=VMEM_LIMIT)


def _mod_spec(row_fn, k, D):
    return pl.BlockSpec((None, 1, D), lambda i, *_: (row_fn(i) * N_MOD + k, 0, 0))


def _rms_mod(x, ng, sc, sh):
    ms = jnp.mean(x * x, axis=-1, keepdims=True)
    return (x * lax.rsqrt(ms + EPS)) * (ng * (1.0 + sc)) + sh


def _sigmoid(x):
    return 0.5 * jnp.tanh(0.5 * x) + 0.5


def _ada_kernel(c_ref, w_ref, b_ref, o_ref):
    c = c_ref[...]
    a = (c * _sigmoid(c)).astype(BF16)
    o_ref[...] = jnp.dot(a, w_ref[...].astype(BF16), preferred_element_type=F32) + b_ref[...]


def _ada(cc, ada_w, ada_b):
    depth, D, N = ada_w.shape
    R = cc.shape[0]
    tn = min(1024, D)
    assert N % tn == 0
    return pl.pallas_call(
        _ada_kernel,
        grid=(depth, N // tn),
        in_specs=[pl.BlockSpec((R, D), lambda l, j: (0, 0)),
                  pl.BlockSpec((None, D, tn), lambda l, j: (l, 0, j)),
                  pl.BlockSpec((None, 1, tn), lambda l, j: (l, 0, j))],
        out_specs=pl.BlockSpec((None, R, tn), lambda l, j: (l, 0, j)),
        out_shape=jax.ShapeDtypeStruct((depth, R, N), F32),
        compiler_params=_params(("arbitrary", "arbitrary")),
        name="ada_mods",
    )(cc, ada_w, ada_b.reshape(depth, 1, N))


def _ffn_kernel(*refs, final):
    if final:
        s_ref, sh_ref, sc_ref, gt_ref, ng_ref, wg_ref, wu_ref, wd_ref, fg_ref, o_ref, h_scr = refs
    else:
        s_ref, sh_ref, sc_ref, gt_ref, ng_ref, wg_ref, wu_ref, wd_ref, o_ref, h_scr = refs
    j = pl.program_id(1)

    @pl.when(j == 0)
    def _():
        h_scr[...] = _rms_mod(s_ref[...], ng_ref[...], sc_ref[...], sh_ref[...]).astype(BF16)
        o_ref[...] = jnp.zeros_like(o_ref)

    h = h_scr[...]
    g = jnp.dot(h, wg_ref[...], preferred_element_type=F32)
    u = jnp.dot(h, wu_ref[...], preferred_element_type=F32)
    a = (g * _sigmoid(g) * u).astype(BF16)
    o_ref[...] += jnp.dot(a, wd_ref[...], preferred_element_type=F32)

    @pl.when(j == pl.num_programs(1) - 1)
    def _():
        y = s_ref[...] + (MACARON_W * gt_ref[...]) * o_ref[...]
        if final:
            y = y * lax.rsqrt(jnp.mean(y * y, axis=-1, keepdims=True) + EPS) * fg_ref[...]
        o_ref[...] = y


def _ffn(s, mods, row_fn, k, ng, wg, wu, wd, l, which, tm, tf, final_g=None):
    M, D = s.shape
    Fd = wg.shape[3]
    assert M % tm == 0 and Fd % tf == 0
    nf = Fd // tf
    in_specs = [pl.BlockSpec((tm, D), lambda i, j: (i, 0)),
                _mod_spec(row_fn, 3 * k, D), _mod_spec(row_fn, 3 * k + 1, D), _mod_spec(row_fn, 3 * k + 2, D),
                pl.BlockSpec((1, D), lambda i, j: (0, 0)),
                pl.BlockSpec((None, None, D, tf), lambda i, j: (l, which, 0, j)),
                pl.BlockSpec((None, None, D, tf), lambda i, j: (l, which, 0, j)),
                pl.BlockSpec((None, None, tf, D), lambda i, j: (l, which, j, 0))]
    args = [s, mods, mods, mods, ng.reshape(1, D), wg, wu, wd]
    if final_g is not None:
        in_specs.append(pl.BlockSpec((1, D), lambda i, j: (0, 0)))
        args.append(final_g.reshape(1, D))
    return pl.pallas_call(
        functools.partial(_ffn_kernel, final=final_g is not None),
        grid=(M // tm, nf),
        in_specs=in_specs,
        out_specs=pl.BlockSpec((tm, D), lambda i, j: (i, 0)),
        out_shape=jax.ShapeDtypeStruct((M, D), F32),
        scratch_shapes=[pltpu.VMEM((tm, D), BF16)],
        compiler_params=_params(("parallel", "arbitrary")),
        name="ffn_half_step",
    )(*args)


def _in_kernel(*refs, prefetch):
    if prefetch:
        refs = refs[1:]
    x_ref, sh_ref, sc_ref, ng_ref, w_ref, o_ref, h_scr = refs

    @pl.when(pl.program_id(1) == 0)
    def _():
        h_scr[...] = _rms_mod(x_ref[...], ng_ref[...], sc_ref[...], sh_ref[...]).astype(BF16)

    o_ref[...] = jnp.dot(h_scr[...], w_ref[...].astype(BF16), preferred_element_type=F32).astype(BF16)


def _in_proj(s, mods, row_fn, ng, w, l, tm, tn, col_blocks=None):
    M, D = s.shape
    N = w.shape[2]
    assert M % tm == 0 and N % tn == 0
    nb = N // tn if col_blocks is None else len(col_blocks)
    prefetch = col_blocks is not None
    if prefetch:
        w_map = lambda i, j, cb: (l, 0, cb[j])
    else:
        w_map = lambda i, j: (l, 0, j)
    grid_spec = pltpu.PrefetchScalarGridSpec(
        num_scalar_prefetch=1 if prefetch else 0,
        grid=(M // tm, nb),
        in_specs=[pl.BlockSpec((tm, D), lambda i, j, *_: (i, 0)),
                  _mod_spec(row_fn, 3, D), _mod_spec(row_fn, 4, D),
                  pl.BlockSpec((1, D), lambda i, j, *_: (0, 0)),
                  pl.BlockSpec((None, D, tn), w_map)],
        out_specs=pl.BlockSpec((tm, tn), lambda i, j, *_: (i, j)),
        scratch_shapes=[pltpu.VMEM((tm, D), BF16)])
    args = [s, mods, mods, ng.reshape(1, D), w]
    if prefetch:
        args = [jnp.asarray(col_blocks, jnp.int32)] + args
    return pl.pallas_call(
        functools.partial(_in_kernel, prefetch=prefetch),
        grid_spec=grid_spec,
        out_shape=jax.ShapeDtypeStruct((M, nb * tn), BF16),
        compiler_params=_params(("parallel", "arbitrary")),
        name="in_proj",
    )(*args)


def _prefix8(A, Bv, row, reverse):
    for s in (1, 2, 4):
        if reverse:
            valid = row < SUBLANES - s
            sh = SUBLANES - s
        else:
            valid = row >= s
            sh = s
        As = jnp.where(valid, pltpu.roll(A, sh, 0), 1.0)
        Bs = jnp.where(valid, pltpu.roll(Bv, sh, 0), 0.0)
        Bv = A * Bs + Bv
        A = A * As
    return A, Bv


def _scan_both(a_scr, b_scr, hf_scr, hb_scr, start, n, hf, hb):
    C = hf_scr.shape[1]
    row = lax.broadcasted_iota(jnp.int32, (SUBLANES, C), 0)
    nblk = n // SUBLANES

    def body(i, carry):
        hf, hb = carry
        bf = pl.multiple_of(start + i * SUBLANES, SUBLANES)
        bb = pl.multiple_of(start + (nblk - 1 - i) * SUBLANES, SUBLANES)
        Af, Bf = _prefix8(a_scr[0, pl.ds(bf, SUBLANES), :], b_scr[0, pl.ds(bf, SUBLANES), :], row, False)
        Ab, Bb = _prefix8(a_scr[1, pl.ds(bb, SUBLANES), :], b_scr[1, pl.ds(bb, SUBLANES), :], row, True)
        Hf = Bf + Af * hf
        Hb = Bb + Ab * hb
        hf_scr[pl.ds(bf, SUBLANES), :] = Hf
        hb_scr[pl.ds(bb, SUBLANES), :] = Hb
        return (jnp.broadcast_to(Hf[SUBLANES - 1:SUBLANES, :], (SUBLANES, C)),
                jnp.broadcast_to(Hb[0:1, :], (SUBLANES, C)))

    return lax.fori_loop(0, nblk, body, (hf, hb), unroll=8)


def _rnn_kernel(*refs, T, Tc, need_ctx):
    if need_ctx:
        (xl_ref, xc_ref, yl_ref, yc_ref, cw_ref, cb_ref, wr_ref, br_ref, wi_ref, bi_ref, lam_ref,
         ol_ref, oc_ref, xp_l, xp_c, u_scr, a_scr, b_scr, hf_scr, hb_scr) = refs
    else:
        (xl_ref, xc_ref, yl_ref, cw_ref, cb_ref, wr_ref, br_ref, wi_ref, bi_ref, lam_ref,
         ol_ref, xp_l, xp_c, u_scr, a_scr, b_scr, hf_scr, hb_scr) = refs
    C = u_scr.shape[1]
    K = cw_ref.shape[0]
    P = SUBLANES

    def conv(x_ref, xp, off, n):
        xp[0:P, :] = jnp.zeros((P, C), F32)
        xp[P + n:2 * P + n, :] = jnp.zeros((P, C), F32)
        xp[P:P + n, :] = x_ref[...].astype(F32)
        acc = jnp.broadcast_to(cb_ref[...], (n, C))
        lo = (K - 1) // 2
        for k in range(K):
            acc = acc + cw_ref[k:k + 1, :] * xp[pl.ds(P - lo + k, n), :]
        u_scr[off:off + n, :] = acc

    conv(xc_ref, xp_c, 0, Tc)
    conv(xl_ref, xp_l, Tc, T)

    tiny = jnp.finfo(F32).tiny
    for d in range(2):
        hk = (-0.5 * RG_C) * jax.nn.softplus(-lam_ref[d:d + 1, :])
        hbr = 0.5 * br_ref[d:d + 1, :]
        hbi = 0.5 * bi_ref[d:d + 1, :]
        for off, n in ((0, Tc), (Tc, T)):
            uh = 0.5 * u_scr[off:off + n, :]
            ub = uh.astype(BF16)
            t_r = jnp.tanh(jnp.dot(ub, wr_ref[d], preferred_element_type=F32) + hbr)
            t_i = jnp.tanh(jnp.dot(ub, wi_ref[d], preferred_element_type=F32) + hbi)
            log_a = t_r * hk + hk
            a = jnp.exp(log_a)
            om = -jnp.tanh(log_a) * (1.0 + a * a)
            root = om * lax.rsqrt(jnp.maximum(om, tiny))
            a_scr[d, off:off + n, :] = a
            b_scr[d, off:off + n, :] = root * (t_i * uh + uh)

    zero = jnp.zeros((SUBLANES, C), F32)
    hf, hb = _scan_both(a_scr, b_scr, hf_scr, hb_scr, 0, Tc, zero, zero)
    _scan_both(a_scr, b_scr, hf_scr, hb_scr, Tc, T, hf, hb)

    ol_ref[...] = (jax.nn.gelu(yl_ref[...].astype(F32)) * (hf_scr[Tc:Tc + T, :] + hb_scr[Tc:Tc + T, :])).astype(BF16)
    if need_ctx:
        oc_ref[...] = (jax.nn.gelu(yc_ref[...].astype(F32)) * (hf_scr[0:Tc, :] + hb_scr[0:Tc, :])).astype(BF16)


def _rnn(zl, zc, col_x, col_y, col_xc, col_yc, cw, cb, wr_bd, br, wi_bd, bi, lam, l, need_ctx, tc):
    B, T, _ = zl.shape
    Tc = zc.shape[1]
    K, C = cw.shape[1:]
    assert C % tc == 0 and T % SUBLANES == 0 and Tc % SUBLANES == 0
    assert all(o % tc == 0 for o in (col_x, col_y, col_xc)) and (col_yc is None or col_yc % tc == 0)
    nt = C // tc
    bx, by, bxc = col_x // tc, col_y // tc, col_xc // tc
    in_specs = [pl.BlockSpec((None, T, tc), lambda b, c: (b, 0, bx + c)),
                pl.BlockSpec((None, Tc, tc), lambda b, c: (b, 0, bxc + c)),
                pl.BlockSpec((None, T, tc), lambda b, c: (b, 0, by + c))]
    args = [zl, zc, zl]
    if need_ctx:
        byc = col_yc // tc
        in_specs.append(pl.BlockSpec((None, Tc, tc), lambda b, c: (b, 0, byc + c)))
        args.append(zc)
    in_specs += [pl.BlockSpec((None, K, tc), lambda b, c: (l, 0, c)),
                 pl.BlockSpec((None, 1, tc), lambda b, c: (l, 0, c)),
                 pl.BlockSpec((None, 2, None, tc, tc), lambda b, c: (l, 0, c, 0, 0)),
                 pl.BlockSpec((None, 2, tc), lambda b, c: (l, 0, c)),
                 pl.BlockSpec((None, 2, None, tc, tc), lambda b, c: (l, 0, c, 0, 0)),
                 pl.BlockSpec((None, 2, tc), lambda b, c: (l, 0, c)),
                 pl.BlockSpec((None, 2, tc), lambda b, c: (l, 0, c))]
    args += [cw, cb.reshape(cb.shape[0], 1, C), wr_bd, br, wi_bd, bi, lam]
    out_specs = [pl.BlockSpec((None, T, tc), lambda b, c: (b, 0, c))]
    out_shape = [jax.ShapeDtypeStruct((B, T, C), BF16)]
    if need_ctx:
        out_specs.append(pl.BlockSpec((None, Tc, tc), lambda b, c: (b, 0, c)))
        out_shape.append(jax.ShapeDtypeStruct((B, Tc, C), BF16))
    Tt = T + Tc
    outs = pl.pallas_call(
        functools.partial(_rnn_kernel, T=T, Tc=Tc, need_ctx=need_ctx),
        grid=(B, nt),
        in_specs=in_specs,
        out_specs=out_specs,
        out_shape=out_shape,
        scratch_shapes=[pltpu.VMEM((T + 2 * SUBLANES, tc), F32), pltpu.VMEM((Tc + 2 * SUBLANES, tc), F32),
                        pltpu.VMEM((Tt, tc), F32), pltpu.VMEM((2, Tt, tc), F32), pltpu.VMEM((2, Tt, tc), F32),
                        pltpu.VMEM((Tt, tc), F32), pltpu.VMEM((Tt, tc), F32)],
        compiler_params=_params(("parallel", "parallel")),
        name="rglru",
    )(*args)
    return (outs[0], outs[1]) if need_ctx else (outs[0], None)


def _conv_kernel(cur_ref, prev_ref, next_ref, w_ref, b_ref, lg_ref, lb_ref, o_ref, xp, sh_scr, cv_scr, wb_scr,
                 *, halo, rb, lt):
    i = pl.program_id(1)
    tt, C = o_ref.shape
    K = w_ref.shape[0]
    half = (K - 1) // 2

    def glu(ref):
        g = ref[...].astype(F32)
        return g[:, :C] * _sigmoid(g[:, C:])

    xp[0:halo, :] = jnp.where(i > 0, glu(prev_ref), 0.0)
    xp[halo:halo + tt, :] = glu(cur_ref)
    xp[halo + tt:2 * halo + tt, :] = jnp.where(i < pl.num_programs(1) - 1, glu(next_ref), 0.0)

    n_sh = tt + 2 * halo - SUBLANES
    for r in range(SUBLANES):
        sh_scr[r] = xp[pl.ds(r, n_sh), :]

    for k in range(K):
        wb_scr[k] = jnp.broadcast_to(w_ref[k:k + 1, :], (SUBLANES, C))

    groups = rb // SUBLANES

    def blk(bi, carry):
        for li in range(C // lt):
            ls = slice(li * lt, (li + 1) * lt)
            bias = jnp.broadcast_to(b_ref[:, ls], (SUBLANES, lt))
            accs = [bias] * groups
            for k in range(K):
                q, r = divmod(halo - half + k, SUBLANES)
                wk = wb_scr[k, :, ls]
                for g in range(groups):
                    base = pl.multiple_of(bi * rb + (q + g) * SUBLANES, SUBLANES)
                    accs[g] = accs[g] + wk * sh_scr[r, pl.ds(base, SUBLANES), ls]
            for g in range(groups):
                cv_scr[pl.ds(pl.multiple_of(bi * rb + g * SUBLANES, SUBLANES), SUBLANES), ls] = accs[g]
        return carry

    lax.fori_loop(0, tt // rb, blk, 0)

    u = cv_scr[...]
    mu = jnp.mean(u, axis=-1, keepdims=True)
    uc = u - mu
    var = jnp.mean(uc * uc, axis=-1, keepdims=True)
    y = uc * lax.rsqrt(var + EPS) * lg_ref[...] + lb_ref[...]
    o_ref[...] = (y * _sigmoid(y)).astype(BF16)


def _conformer(z, col_g, w, b, lg, lb, l, tt):
    B, T, _ = z.shape
    K, C = w.shape[1:]
    halo = 16
    assert (K - 1) // 2 <= halo and T % tt == 0 and tt % halo == 0 and col_g % (2 * C) == 0
    bg = col_g // (2 * C)
    th = tt // halo
    nh = T // halo
    rb = min(32, tt)
    lt = min(256, C)
    vec = lambda v: v.reshape(v.shape[0], 1, C)
    vspec = pl.BlockSpec((None, 1, C), lambda b_, i: (l, 0, 0))
    return pl.pallas_call(
        functools.partial(_conv_kernel, halo=halo, rb=rb, lt=lt),
        grid=(B, T // tt),
        in_specs=[pl.BlockSpec((None, tt, 2 * C), lambda b_, i: (b_, i, bg)),
                  pl.BlockSpec((None, halo, 2 * C), lambda b_, i: (b_, jnp.maximum(i * th - 1, 0), bg)),
                  pl.BlockSpec((None, halo, 2 * C), lambda b_, i: (b_, jnp.minimum((i + 1) * th, nh - 1), bg)),
                  pl.BlockSpec((None, K, C), lambda b_, i: (l, 0, 0)),
                  vspec, vspec, vspec],
        out_specs=pl.BlockSpec((None, tt, C), lambda b_, i: (b_, i, 0)),
        out_shape=jax.ShapeDtypeStruct((B, T, C), BF16),
        scratch_shapes=[pltpu.VMEM((tt + 2 * halo, C), F32),
                        pltpu.VMEM((SUBLANES, tt + 2 * halo - SUBLANES, C), F32),
                        pltpu.VMEM((tt, C), F32),
                        pltpu.VMEM((K, SUBLANES, C), F32)],
        compiler_params=_params(("parallel", "parallel")),
        name="conformer_conv",
    )(z, z, z, w, vec(b), vec(lg), vec(lb))


def _rope(x, cs, sn, quarter):
    lane = lax.broadcasted_iota(jnp.int32, x.shape, 1)
    partner = jnp.where(lane % (2 * quarter) < quarter,
                        pltpu.roll(x, x.shape[1] - quarter, 1), pltpu.roll(x, quarter, 1))
    return x * cs + partner * sn


def _attn_kernel(*refs, n_src, rope, lam_init, scale, hd, quarter, rs):
    it = iter(refs)
    q_ref = next(it)
    kv = [(next(it), next(it)) for _ in range(n_src)]
    if rope:
        cq_ref, sq_ref, ck_ref, sk_ref = next(it), next(it), next(it), next(it)
    dl_ref, g_ref, o_ref, k_all, v_ext = next(it), next(it), next(it), next(it), next(it)
    i = pl.program_id(2)
    tq, W = q_ref.shape
    hp = W // hd

    @pl.when(i == 0)
    def _():
        off = 0
        for s in range(n_src):
            k_ref, v_ref = kv[s]
            n = k_ref.shape[0]
            if rope and s == 0:
                k_all[off:off + n, :] = _rope(k_ref[...].astype(F32), ck_ref[...], sk_ref[...], quarter).astype(BF16)
            else:
                k_all[off:off + n, :] = k_ref[...]
            for h in range(hp):
                v_ext[h, off:off + n, 0:hd] = v_ref[:, h * hd:(h + 1) * hd]
                v_ext[h, off:off + n, hd:2 * hd] = jnp.ones((n, hd), BF16)
            off += n

    q = q_ref[...].astype(F32)
    if rope:
        q = _rope(q, cq_ref[...], sq_ref[...], quarter)
    qb = (q * (scale * math.log2(math.e))).astype(BF16)
    keys = k_all[...]

    lq = dl_ref[...]
    lam = (jnp.exp(jnp.sum(lq[0:1] * lq[1:2], axis=-1, keepdims=True))
           - jnp.exp(jnp.sum(lq[2:3] * lq[3:4], axis=-1, keepdims=True)) + lam_init)

    lane = lax.broadcasted_iota(jnp.int32, (rs, W), 1)
    half = hd // 2

    def scores(r, ch):
        lo = ch * half
        qr = qb[r * rs:(r + 1) * rs, :]
        qc = jnp.where((lane >= lo) & (lane < lo + half), qr, jnp.zeros_like(qr))
        return lax.dot_general(qc, keys, (((1,), (1,)), ((), ())), preferred_element_type=F32)

    def weights(s):
        return jnp.exp2(s - jnp.max(s, axis=-1, keepdims=True)).astype(BF16)

    def attend(ch, e):
        ne = jnp.dot(e, v_ext[ch // 2], preferred_element_type=F32)
        return ne[:, 0:hd] / ne[:, hd:2 * hd]

    chains = [(r, ch) for r in range(tq // rs) for ch in range(2 * hp)]
    s_cur = scores(*chains[0])
    e_prev = None
    comps = []
    for n in range(len(chains)):
        s_next = scores(*chains[n + 1]) if n + 1 < len(chains) else None
        e_cur = weights(s_cur)
        if e_prev is not None:
            comps.append(attend(chains[n - 1][1], e_prev))
        e_prev, s_cur = e_cur, s_next
    comps.append(attend(chains[-1][1], e_prev))

    for n in range(0, len(chains), 2):
        r, h = chains[n][0], chains[n][1] // 2
        o = comps[n] - lam * comps[n + 1]
        y = o * lax.rsqrt(jnp.mean(o * o, axis=-1, keepdims=True) + EPS) * g_ref[...] * (1.0 - lam_init)
        o_ref[r * rs:(r + 1) * rs, h * hd:(h + 1) * hd] = y.astype(BF16)


def _attention(zq, srcs, col_q, col_ks, col_vs, tabs, da_lam, subln_g, l, lam_init, n_heads, tq):
    B, Tq, _ = zq.shape
    hd = subln_g.shape[1]
    W = min(MXU_WIDTH, n_heads * hd)
    hp = W // hd
    assert hd == LANES and n_heads % hp == 0 and col_q % W == 0 and Tq % tq == 0
    assert all(ck % W == 0 for ck in col_ks) and all(cv % W == 0 for cv in col_vs)
    bq = col_q // W
    rope = tabs is not None
    in_specs = [pl.BlockSpec((None, tq, W), lambda b, h, i: (b, i, bq + h))]
    args = [zq]
    Tk_all = 0
    for zs, ck, cv in zip(srcs, col_ks, col_vs):
        Tk = zs.shape[1]
        Tk_all += Tk
        bk, bv = ck // W, cv // W
        in_specs += [pl.BlockSpec((None, Tk, W), lambda b, h, i, bk=bk: (b, 0, bk + h)),
                     pl.BlockSpec((None, Tk, W), lambda b, h, i, bv=bv: (b, 0, bv + h))]
        args += [zs, zs]
    if rope:
        cs, sn = tabs
        in_specs += [pl.BlockSpec((tq, W), lambda b, h, i: (i, 0)), pl.BlockSpec((tq, W), lambda b, h, i: (i, 0)),
                     pl.BlockSpec((Tq, W), lambda b, h, i: (0, 0)), pl.BlockSpec((Tq, W), lambda b, h, i: (0, 0))]
        args += [cs, sn, cs, sn]
    d_qk = da_lam.shape[2]
    in_specs += [pl.BlockSpec((None,) + da_lam.shape[1:], lambda b, h, i: (l, 0, 0)),
                 pl.BlockSpec((None, 1, hd), lambda b, h, i: (l, 0, 0))]
    args += [da_lam, subln_g.reshape(subln_g.shape[0], 1, hd)]
    return pl.pallas_call(
        functools.partial(_attn_kernel, n_src=len(srcs), rope=rope, lam_init=lam_init, scale=d_qk ** -0.5,
                          hd=hd, quarter=d_qk // 4, rs=min(tq, 256)),
        grid=(B, n_heads // hp, Tq // tq),
        in_specs=in_specs,
        out_specs=pl.BlockSpec((None, tq, W), lambda b, h, i: (b, i, h)),
        out_shape=jax.ShapeDtypeStruct((B, Tq, n_heads * hd), BF16),
        scratch_shapes=[pltpu.VMEM((Tk_all, W), BF16), pltpu.VMEM((hp, Tk_all, 2 * hd), BF16)],
        compiler_params=_params(("parallel", "parallel", "arbitrary")),
        name="diff_attention",
    )(*args)


def _merge_kernel(*refs, nsplit):
    r_ref, c_ref, a_ref = refs[0:3]
    z_refs = refs[3:3 + 3 * nsplit]
    x_ref, gt_ref, w0_ref, w1_ref, w2_ref, wo_ref, o_ref = refs[3 + 3 * nsplit:]
    gw = z_refs[0].shape[1]
    ins = (r_ref[...], c_ref[...], a_ref[...])
    ws = (w0_ref, w1_ref, w2_ref)
    out = None
    for p in range(nsplit):
        cols = slice(p * gw, (p + 1) * gw)
        m = None
        for j in range(3):
            gate = _sigmoid(z_refs[j * nsplit + p][...].astype(F32))
            t = gate * jnp.dot(ins[j], ws[j][:, cols], preferred_element_type=F32)
            m = t if m is None else m + t
        t = jnp.dot(m.astype(BF16), wo_ref[cols, :], preferred_element_type=F32)
        out = t if out is None else out + t
    o_ref[...] = x_ref[...] + gt_ref[...] * out


def _merge(rnn_g, conv_g, att_g, z, col_z, s, mods, row_fn, w0, w1, w2, wo, l, tm):
    M, D = s.shape
    gw = math.gcd(col_z, D)
    nsplit = D // gw
    bz = col_z // gw
    assert M % tm == 0 and gw % LANES == 0

    def act_spec(a):
        return pl.BlockSpec((tm, a.shape[1]), lambda i: (i, 0))

    def w_spec(w):
        return pl.BlockSpec((None,) + w.shape[1:], lambda i: (l, 0, 0), pipeline_mode=pl.Buffered(1))

    z_specs = [pl.BlockSpec((tm, gw), lambda i, blk=bz + j * nsplit + p: (i, blk))
               for j in range(3) for p in range(nsplit)]
    return pl.pallas_call(
        functools.partial(_merge_kernel, nsplit=nsplit),
        grid=(M // tm,),
        in_specs=[act_spec(rnn_g), act_spec(conv_g), act_spec(att_g)] + z_specs + [
            pl.BlockSpec((tm, D), lambda i: (i, 0)),
            _mod_spec(row_fn, 5, D),
            w_spec(w0), w_spec(w1), w_spec(w2), w_spec(wo)],
        out_specs=pl.BlockSpec((tm, D), lambda i: (i, 0)),
        out_shape=jax.ShapeDtypeStruct((M, D), F32),
        compiler_params=_params(("parallel",)),
        name="merge_out_proj",
    )(rnn_g, conv_g, att_g, *([z] * (3 * nsplit)), s, mods, w0, w1, w2, wo)


def _rope_tables(T, d_qk, width):
    n = d_qk // 4
    rows = T // GRID_W
    row = jnp.repeat(jnp.arange(rows, dtype=jnp.int32), GRID_W).astype(F32)
    col = jnp.tile(jnp.arange(GRID_W, dtype=jnp.int32), rows).astype(F32)
    inv = ROPE_BASE ** (-jnp.arange(n, dtype=F32) * 2.0 / (2 * n))
    ar, ac = row[:, None] * inv, col[:, None] * inv
    cs = jnp.concatenate([jnp.cos(ar), jnp.cos(ar), jnp.cos(ac), jnp.cos(ac)], axis=1)
    sn = jnp.concatenate([-jnp.sin(ar), jnp.sin(ar), -jnp.sin(ac), jnp.sin(ac)], axis=1)
    return jnp.tile(cs, (1, width // d_qk)), jnp.tile(sn, (1, width // d_qk))


def _block_diag(w, tc):
    depth, two, nb, bw, _ = w.shape
    per = tc // bw
    w6 = w.reshape(depth, two, nb // per, per, bw, bw)
    eye = jnp.eye(per, dtype=w.dtype)
    bd = jnp.einsum('ldtpij,pq->ldtpiqj', w6, eye)
    return bd.reshape(depth, two, nb // per, tc, tc).astype(BF16)


def kernel(x, c, ctx, c_ctx, ada_w, ada_b, norm_g, ffn_w_gate, ffn_w_up, ffn_w_down, w_in, rnn_conv_w, rnn_conv_b, rg_w_r, rg_b_r, rg_w_i, rg_b_i, rg_lam, rnn_w_out, cv_dw_w, cv_dw_b, cv_ln_g, cv_ln_b, cv_w_out, da_lam, da_subln_g, da_w_o, w_out, final_g):
    B, T, D = x.shape
    Tc = ctx.shape[1]
    depth = ada_w.shape[0]
    d_rnn = rnn_conv_w.shape[2]
    d_conv = cv_dw_w.shape[2]
    d_qk = da_lam.shape[2]
    d_v = da_subln_g.shape[1]
    d_att = da_w_o.shape[1]
    n_heads = d_att // d_v
    d_q = n_heads * 2 * d_qk
    splits = (d_rnn, d_rnn, 2 * d_conv, d_q, d_q, d_att, 3 * D)
    assert sum(splits) == w_in.shape[2] and 2 * d_qk == d_v
    offs = [0]
    for w_ in splits[:-1]:
        offs.append(offs[-1] + w_)
    col_x, col_y, col_g, col_q, col_k, col_v, col_z = offs

    tl = _tiles(T, T)
    tcx = _tiles(Tc, B * Tc)
    tc_rnn = min(tl["tc_rnn"], d_rnn)
    att_w = min(MXU_WIDTH, n_heads * d_v)

    R = -(-(B + 1) // SUBLANES) * SUBLANES
    cc = jnp.zeros((R, D), F32).at[:B].set(c).at[B].set(c_ctx)
    mods_all = _ada(cc, ada_w, ada_b).reshape(depth, R * N_MOD, 1, D)

    tabs = _rope_tables(T, d_qk, att_w)

    tn_in = math.gcd(math.gcd(d_rnn, col_k), math.gcd(d_q, d_att))
    ctx_blocks = tuple(range(col_x // tn_in, (col_x + d_rnn) // tn_in)) + \
        tuple(range(col_k // tn_in, (col_v + d_att) // tn_in))

    wg, wu, wd = ffn_w_gate.astype(BF16), ffn_w_up.astype(BF16), ffn_w_down.astype(BF16)
    wr_bd = _block_diag(rg_w_r, tc_rnn)
    wi_bd = _block_diag(rg_w_i, tc_rnn)
    w_rnn, w_cv, w_da, w_o = (rnn_w_out.astype(BF16), cv_w_out.astype(BF16), da_w_o.astype(BF16),
                              w_out.astype(BF16))

    def lat_row(tm):
        per = T // tm
        return lambda i: i // per

    def ctx_row(tm):
        return lambda i: B

    xl = x.reshape(B * T, D)
    xc = ctx.reshape(B * Tc, D)
    for l in range(depth):
        need_ctx = l < depth - 1
        last = l == depth - 1
        lam_init = 0.8 - 0.6 * math.exp(-0.3 * l)
        mods = mods_all[l]

        xl = _ffn(xl, mods, lat_row(tl["tm_ffn"]), 0, norm_g[l, 0], wg, wu, wd, l, 0, tl["tm_ffn"], tl["tf"])
        xc = _ffn(xc, mods, ctx_row(tcx["tm_ffn"]), 0, norm_g[l, 0], wg, wu, wd, l, 0, tcx["tm_ffn"], tcx["tf"])

        zl = _in_proj(xl, mods, lat_row(tl["tm_in"]), norm_g[l, 1], w_in, l,tl["tm_in"], tn_in)
        if need_ctx:
            zc = _in_proj(xc, mods, ctx_row(tcx["tm_in"]), norm_g[l, 1], w_in, l,tcx["tm_in"], tn_in)
            cxc, cyc, ckc, cvc = col_x, col_y, col_k, col_v
        else:
            zc = _in_proj(xc, mods, ctx_row(tcx["tm_in"]), norm_g[l, 1], w_in, l,tcx["tm_in"], tn_in,
                          col_blocks=ctx_blocks)
            cxc, cyc, ckc, cvc = 0, None, d_rnn, d_rnn + d_q
        zl3 = zl.reshape(B, T, -1)
        zc3 = zc.reshape(B, Tc, -1)

        rnn_l, rnn_c = _rnn(zl3, zc3, col_x, col_y, cxc, cyc, rnn_conv_w, rnn_conv_b, wr_bd, rg_b_r, wi_bd,
                            rg_b_i, rg_lam, l, need_ctx, tc_rnn)
        conv_l = _conformer(zl3, col_g, cv_dw_w, cv_dw_b, cv_ln_g, cv_ln_b, l, tl["tt_conv"])
        att_l = _attention(zl3, [zl3, zc3], col_q, [col_k, ckc], [col_v, cvc], tabs, da_lam, da_subln_g, l,
                           lam_init, n_heads, tl["tq"])
        xl = _merge(rnn_l.reshape(B * T, -1), conv_l.reshape(B * T, -1), att_l.reshape(B * T, -1), zl, col_z,
                    xl, mods, lat_row(tl["tm_merge"]), w_rnn, w_cv, w_da, w_o, l, tl["tm_merge"])
        xl = _ffn(xl, mods, lat_row(tl["tm_ffn"]), 2, norm_g[l, 2], wg, wu, wd, l, 1, tl["tm_ffn"], tl["tf"],
                  final_g=final_g if last else None)
        if need_ctx:
            conv_c = _conformer(zc3, col_g, cv_dw_w, cv_dw_b, cv_ln_g, cv_ln_b, l, tcx["tt_conv"])
            att_c = _attention(zc3, [zc3], col_q, [col_k], [col_v], None, da_lam, da_subln_g, l,
                               lam_init, n_heads, tcx["tq"])
            xc = _merge(rnn_c.reshape(B * Tc, -1), conv_c.reshape(B * Tc, -1), att_c.reshape(B * Tc, -1), zc, col_z,
                        xc, mods, ctx_row(tcx["tm_merge"]), w_rnn, w_cv, w_da, w_o, l, tcx["tm_merge"])
            xc = _ffn(xc, mods, ctx_row(tcx["tm_ffn"]), 2, norm_g[l, 2], wg, wu, wd, l, 1,
                      tcx["tm_ffn"], tcx["tf"])
    return xl.reshape(B, T, D)
```

```python
import functools
import math

import jax
import jax.numpy as jnp
from jax import lax
from jax.experimental import pallas as pl
from jax.experimental.pallas import tpu as pltpu

F32 = jnp.float32
BF16 = jnp.bfloat16

EPS = 1e-6
GRID_W = 64
ROPE_BASE = 10000.0
RG_C = 8.0
MACARON_W = 0.5
N_MOD = 9

LANES = 128
SUBLANES = 8
MXU_WIDTH = 256
VMEM_LIMIT = 56 * 1024 * 1024


def _tiles(T, rows):
    return dict(
        tm_ffn=min(1024, rows), tf=512, tm_in=min(1024, rows), tm_merge=min(256, rows),
        tt_conv=min(256, T), tq=min(1024, T), tc_rnn=256)


def _params(sem):
    return pltpu.CompilerParams(dimension_semantics=sem, vmem_limit_bytes=VMEM_LIMIT)


def _mod_spec(row_fn, k, D):
    return pl.BlockSpec((None, 1, D), lambda i, *_: (row_fn(i) * N_MOD + k, 0, 0))


def _rms_mod(x, ng, sc, sh):
    ms = jnp.mean(x * x, axis=-1, keepdims=True)
    return (x * lax.rsqrt(ms + EPS)) * (ng * (1.0 + sc)) + sh


def _sigmoid(x):
    return 0.5 * jnp.tanh(0.5 * x) + 0.5


def _ada_kernel(c_ref, w_ref, b_ref, o_ref):
    c = c_ref[...]
    a = (c * _sigmoid(c)).astype(BF16)
    o_ref[...] = jnp.dot(a, w_ref[...].astype(BF16), preferred_element_type=F32) + b_ref[...]


def _ada(cc, ada_w, ada_b):
    depth, D, N = ada_w.shape
    R = cc.shape[0]
    tn = min(1024, D)
    assert N % tn == 0
    return pl.pallas_call(
        _ada_kernel,
        grid=(depth, N // tn),
        in_specs=[pl.BlockSpec((R, D), lambda l, j: (0, 0)),
                  pl.BlockSpec((None, D, tn), lambda l, j: (l, 0, j)),
                  pl.BlockSpec((None, 1, tn), lambda l, j: (l, 0, j))],
        out_specs=pl.BlockSpec((None, R, tn), lambda l, j: (l, 0, j)),
        out_shape=jax.ShapeDtypeStruct((depth, R, N), F32),
        compiler_params=_params(("arbitrary", "arbitrary")),
        name="ada_mods",
    )(cc, ada_w, ada_b.reshape(depth, 1, N))


def _ffn_kernel(*refs, final):
    if final:
        s_ref, sh_ref, sc_ref, gt_ref, ng_ref, wg_ref, wu_ref, wd_ref, fg_ref, o_ref, h_scr = refs
    else:
        s_ref, sh_ref, sc_ref, gt_ref, ng_ref, wg_ref, wu_ref, wd_ref, o_ref, h_scr = refs
    j = pl.program_id(1)

    @pl.when(j == 0)
    def _():
        h_scr[...] = _rms_mod(s_ref[...], ng_ref[...], sc_ref[...], sh_ref[...]).astype(BF16)
        o_ref[...] = jnp.zeros_like(o_ref)

    h = h_scr[...]
    g = jnp.dot(h, wg_ref[...], preferred_element_type=F32)
    u = jnp.dot(h, wu_ref[...], preferred_element_type=F32)
    a = (g * _sigmoid(g) * u).astype(BF16)
    o_ref[...] += jnp.dot(a, wd_ref[...], preferred_element_type=F32)

    @pl.when(j == pl.num_programs(1) - 1)
    def _():
        y = s_ref[...] + (MACARON_W * gt_ref[...]) * o_ref[...]
        if final:
            y = y * lax.rsqrt(jnp.mean(y * y, axis=-1, keepdims=True) + EPS) * fg_ref[...]
        o_ref[...] = y


def _ffn(s, mods, row_fn, k, ng, wg, wu, wd, l, which, tm, tf, final_g=None):
    M, D = s.shape
    Fd = wg.shape[3]
    assert M % tm == 0 and Fd % tf == 0
    nf = Fd // tf
    in_specs = [pl.BlockSpec((tm, D), lambda i, j: (i, 0), pipeline_mode=pl.Buffered(1)),
                _mod_spec(row_fn, 3 * k, D), _mod_spec(row_fn, 3 * k + 1, D), _mod_spec(row_fn, 3 * k + 2, D),
                pl.BlockSpec((1, D), lambda i, j: (0, 0)),
                pl.BlockSpec((None, None, D, tf), lambda i, j: (l, which, 0, j)),
                pl.BlockSpec((None, None, D, tf), lambda i, j: (l, which, 0, j)),
                pl.BlockSpec((None, None, tf, D), lambda i, j: (l, which, j, 0))]
    args = [s, mods, mods, mods, ng.reshape(1, D), wg, wu, wd]
    if final_g is not None:
        in_specs.append(pl.BlockSpec((1, D), lambda i, j: (0, 0)))
        args.append(final_g.reshape(1, D))
    return pl.pallas_call(
        functools.partial(_ffn_kernel, final=final_g is not None),
        grid=(M // tm, nf),
        in_specs=in_specs,
        out_specs=pl.BlockSpec((tm, D), lambda i, j: (i, 0)),
        out_shape=jax.ShapeDtypeStruct((M, D), F32),
        scratch_shapes=[pltpu.VMEM((tm, D), BF16)],
        compiler_params=_params(("parallel", "arbitrary")),
        name="ffn_half_step",
    )(*args)


def _in_kernel(*refs, prefetch):
    if prefetch:
        refs = refs[1:]
    x_ref, sh_ref, sc_ref, ng_ref, w_ref, o_ref, h_scr = refs

    @pl.when(pl.program_id(1) == 0)
    def _():
        h_scr[...] = _rms_mod(x_ref[...], ng_ref[...], sc_ref[...], sh_ref[...]).astype(BF16)

    o_ref[...] = jnp.dot(h_scr[...], w_ref[...].astype(BF16), preferred_element_type=F32).astype(BF16)


def _in_proj(s, mods, row_fn, ng, w, l, tm, tn, col_blocks=None):
    M, D = s.shape
    N = w.shape[2]
    assert M % tm == 0 and N % tn == 0
    nb = N // tn if col_blocks is None else len(col_blocks)
    prefetch = col_blocks is not None
    if prefetch:
        w_map = lambda i, j, cb: (l, 0, cb[j])
    else:
        w_map = lambda i, j: (l, 0, j)
    grid_spec = pltpu.PrefetchScalarGridSpec(
        num_scalar_prefetch=1 if prefetch else 0,
        grid=(M // tm, nb),
        in_specs=[pl.BlockSpec((tm, D), lambda i, j, *_: (i, 0)),
                  _mod_spec(row_fn, 3, D), _mod_spec(row_fn, 4, D),
                  pl.BlockSpec((1, D), lambda i, j, *_: (0, 0)),
                  pl.BlockSpec((None, D, tn), w_map)],
        out_specs=pl.BlockSpec((tm, tn), lambda i, j, *_: (i, j)),
        scratch_shapes=[pltpu.VMEM((tm, D), BF16)])
    args = [s, mods, mods, ng.reshape(1, D), w]
    if prefetch:
        args = [jnp.asarray(col_blocks, jnp.int32)] + args
    return pl.pallas_call(
        functools.partial(_in_kernel, prefetch=prefetch),
        grid_spec=grid_spec,
        out_shape=jax.ShapeDtypeStruct((M, nb * tn), BF16),
        compiler_params=_params(("parallel", "arbitrary")),
        name="in_proj",
    )(*args)


def _prefix8(A, Bv, row, reverse):
    for s in (1, 2, 4):
        if reverse:
            valid = row < SUBLANES - s
            sh = SUBLANES - s
        else:
            valid = row >= s
            sh = s
        As = jnp.where(valid, pltpu.roll(A, sh, 0), 1.0)
        Bs = jnp.where(valid, pltpu.roll(Bv, sh, 0), 0.0)
        Bv = A * Bs + Bv
        A = A * As
    return A, Bv


def _scan_both(a_scr, b_scr, hf_scr, hb_scr, start, n, hf, hb):
    C = hf_scr.shape[1]
    row = lax.broadcasted_iota(jnp.int32, (SUBLANES, C), 0)
    nblk = n // SUBLANES

    def body(i, carry):
        hf, hb = carry
        bf = pl.multiple_of(start + i * SUBLANES, SUBLANES)
        bb = pl.multiple_of(start + (nblk - 1 - i) * SUBLANES, SUBLANES)
        Af, Bf = _prefix8(a_scr[0, pl.ds(bf, SUBLANES), :], b_scr[0, pl.ds(bf, SUBLANES), :], row, False)
        Ab, Bb = _prefix8(a_scr[1, pl.ds(bb, SUBLANES), :], b_scr[1, pl.ds(bb, SUBLANES), :], row, True)
        Hf = Bf + Af * hf
        Hb = Bb + Ab * hb
        hf_scr[pl.ds(bf, SUBLANES), :] = Hf
        hb_scr[pl.ds(bb, SUBLANES), :] = Hb
        return (jnp.broadcast_to(Hf[SUBLANES - 1:SUBLANES, :], (SUBLANES, C)),
                jnp.broadcast_to(Hb[0:1, :], (SUBLANES, C)))

    return lax.fori_loop(0, nblk, body, (hf, hb), unroll=8)


def _rnn_kernel(*refs, T, Tc, need_ctx):
    if need_ctx:
        (xl_ref, xc_ref, yl_ref, yc_ref, cw_ref, cb_ref, wr_ref, br_ref, wi_ref, bi_ref, lam_ref,
         ol_ref, oc_ref, xp_l, xp_c, u_scr, a_scr, b_scr, hf_scr, hb_scr) = refs
    else:
        (xl_ref, xc_ref, yl_ref, cw_ref, cb_ref, wr_ref, br_ref, wi_ref, bi_ref, lam_ref,
         ol_ref, xp_l, xp_c, u_scr, a_scr, b_scr, hf_scr, hb_scr) = refs
    C = u_scr.shape[1]
    K = cw_ref.shape[0]
    P = SUBLANES

    def conv(x_ref, xp, off, n):
        xp[0:P, :] = jnp.zeros((P, C), F32)
        xp[P + n:2 * P + n, :] = jnp.zeros((P, C), F32)
        xp[P:P + n, :] = x_ref[...].astype(F32)
        acc = jnp.broadcast_to(cb_ref[...], (n, C))
        lo = (K - 1) // 2
        for k in range(K):
            acc = acc + cw_ref[k:k + 1, :] * xp[pl.ds(P - lo + k, n), :]
        u_scr[off:off + n, :] = acc

    conv(xc_ref, xp_c, 0, Tc)
    conv(xl_ref, xp_l, Tc, T)

    tiny = jnp.finfo(F32).tiny
    for d in range(2):
        hk = (-0.5 * RG_C) * jax.nn.softplus(-lam_ref[d:d + 1, :])
        hbr = 0.5 * br_ref[d:d + 1, :]
        hbi = 0.5 * bi_ref[d:d + 1, :]
        for off, n in ((0, Tc), (Tc, T)):
            uh = 0.5 * u_scr[off:off + n, :]
            ub = uh.astype(BF16)
            t_r = jnp.tanh(jnp.dot(ub, wr_ref[d], preferred_element_type=F32) + hbr)
            t_i = jnp.tanh(jnp.dot(ub, wi_ref[d], preferred_element_type=F32) + hbi)
            log_a = t_r * hk + hk
            a = jnp.exp(log_a)
            om = -jnp.tanh(log_a) * (1.0 + a * a)
            root = om * lax.rsqrt(jnp.maximum(om, tiny))
            a_scr[d, off:off + n, :] = a
            b_scr[d, off:off + n, :] = root * (t_i * uh + uh)

    zero = jnp.zeros((SUBLANES, C), F32)
    hf, hb = _scan_both(a_scr, b_scr, hf_scr, hb_scr, 0, Tc, zero, zero)
    _scan_both(a_scr, b_scr, hf_scr, hb_scr, Tc, T, hf, hb)

    ol_ref[...] = (jax.nn.gelu(yl_ref[...].astype(F32)) * (hf_scr[Tc:Tc + T, :] + hb_scr[Tc:Tc + T, :])).astype(BF16)
    if need_ctx:
        oc_ref[...] = (jax.nn.gelu(yc_ref[...].astype(F32)) * (hf_scr[0:Tc, :] + hb_scr[0:Tc, :])).astype(BF16)


def _rnn(zl, zc, col_x, col_y, col_xc, col_yc, cw, cb, wr_bd, br, wi_bd, bi, lam, l, need_ctx, tc):
    B, T, _ = zl.shape
    Tc = zc.shape[1]
    K, C = cw.shape[1:]
    assert C % tc == 0 and T % SUBLANES == 0 and Tc % SUBLANES == 0
    assert all(o % tc == 0 for o in (col_x, col_y, col_xc)) and (col_yc is None or col_yc % tc == 0)
    nt = C // tc
    bx, by, bxc = col_x // tc, col_y // tc, col_xc // tc
    in_specs = [pl.BlockSpec((None, T, tc), lambda b, c: (b, 0, bx + c)),
                pl.BlockSpec((None, Tc, tc), lambda b, c: (b, 0, bxc + c)),
                pl.BlockSpec((None, T, tc), lambda b, c: (b, 0, by + c))]
    args = [zl, zc, zl]
    if need_ctx:
        byc = col_yc // tc
        in_specs.append(pl.BlockSpec((None, Tc, tc), lambda b, c: (b, 0, byc + c)))
        args.append(zc)
    in_specs += [pl.BlockSpec((None, K, tc), lambda b, c: (l, 0, c)),
                 pl.BlockSpec((None, 1, tc), lambda b, c: (l, 0, c)),
                 pl.BlockSpec((None, 2, None, tc, tc), lambda b, c: (l, 0, c, 0, 0)),
                 pl.BlockSpec((None, 2, tc), lambda b, c: (l, 0, c)),
                 pl.BlockSpec((None, 2, None, tc, tc), lambda b, c: (l, 0, c, 0, 0)),
                 pl.BlockSpec((None, 2, tc), lambda b, c: (l, 0, c)),
                 pl.BlockSpec((None, 2, tc), lambda b, c: (l, 0, c))]
    args += [cw, cb.reshape(cb.shape[0], 1, C), wr_bd, br, wi_bd, bi, lam]
    out_specs = [pl.BlockSpec((None, T, tc), lambda b, c: (b, 0, c))]
    out_shape = [jax.ShapeDtypeStruct((B, T, C), BF16)]
    if need_ctx:
        out_specs.append(pl.BlockSpec((None, Tc, tc), lambda b, c: (b, 0, c)))
        out_shape.append(jax.ShapeDtypeStruct((B, Tc, C), BF16))
    Tt = T + Tc
    outs = pl.pallas_call(
        functools.partial(_rnn_kernel, T=T, Tc=Tc, need_ctx=need_ctx),
        grid=(B, nt),
        in_specs=in_specs,
        out_specs=out_specs,
        out_shape=out_shape,
        scratch_shapes=[pltpu.VMEM((T + 2 * SUBLANES, tc), F32), pltpu.VMEM((Tc + 2 * SUBLANES, tc), F32),
                        pltpu.VMEM((Tt, tc), F32), pltpu.VMEM((2, Tt, tc), F32), pltpu.VMEM((2, Tt, tc), F32),
                        pltpu.VMEM((Tt, tc), F32), pltpu.VMEM((Tt, tc), F32)],
        compiler_params=_params(("parallel", "parallel")),
        name="rglru",
    )(*args)
    return (outs[0], outs[1]) if need_ctx else (outs[0], None)


def _conv_kernel(cur_ref, prev_ref, next_ref, w_ref, b_ref, lg_ref, lb_ref, o_ref, xp, sh_scr, cv_scr, wb_scr,
                 *, halo, rb, lt):
    i = pl.program_id(1)
    tt, C = o_ref.shape
    K = w_ref.shape[0]
    half = (K - 1) // 2

    def glu(ref):
        g = ref[...].astype(F32)
        return g[:, :C] * _sigmoid(g[:, C:])

    xp[0:halo, :] = jnp.where(i > 0, glu(prev_ref), 0.0)
    xp[halo:halo + tt, :] = glu(cur_ref)
    xp[halo + tt:2 * halo + tt, :] = jnp.where(i < pl.num_programs(1) - 1, glu(next_ref), 0.0)

    n_sh = tt + 2 * halo - SUBLANES
    for r in range(SUBLANES):
        sh_scr[r] = xp[pl.ds(r, n_sh), :]

    for k in range(K):
        wb_scr[k] = jnp.broadcast_to(w_ref[k:k + 1, :], (SUBLANES, C))

    groups = rb // SUBLANES

    def blk(bi, carry):
        for li in range(C // lt):
            ls = slice(li * lt, (li + 1) * lt)
            bias = jnp.broadcast_to(b_ref[:, ls], (SUBLANES, lt))
            accs = [bias] * groups
            for k in range(K):
                q, r = divmod(halo - half + k, SUBLANES)
                wk = wb_scr[k, :, ls]
                for g in range(groups):
                    base = pl.multiple_of(bi * rb + (q + g) * SUBLANES, SUBLANES)
                    accs[g] = accs[g] + wk * sh_scr[r, pl.ds(base, SUBLANES), ls]
            for g in range(groups):
                cv_scr[pl.ds(pl.multiple_of(bi * rb + g * SUBLANES, SUBLANES), SUBLANES), ls] = accs[g]
        return carry

    lax.fori_loop(0, tt // rb, blk, 0)

    u = cv_scr[...]
    mu = jnp.mean(u, axis=-1, keepdims=True)
    uc = u - mu
    var = jnp.mean(uc * uc, axis=-1, keepdims=True)
    y = uc * lax.rsqrt(var + EPS) * lg_ref[...] + lb_ref[...]
    o_ref[...] = (y * _sigmoid(y)).astype(BF16)


def _conformer(z, col_g, w, b, lg, lb, l, tt):
    B, T, _ = z.shape
    K, C = w.shape[1:]
    halo = 16
    assert (K - 1) // 2 <= halo and T % tt == 0 and tt % halo == 0 and col_g % (2 * C) == 0
    bg = col_g // (2 * C)
    th = tt // halo
    nh = T // halo
    rb = min(32, tt)
    lt = min(256, C)
    vec = lambda v: v.reshape(v.shape[0], 1, C)
    vspec = pl.BlockSpec((None, 1, C), lambda b_, i: (l, 0, 0))
    return pl.pallas_call(
        functools.partial(_conv_kernel, halo=halo, rb=rb, lt=lt),
        grid=(B, T // tt),
        in_specs=[pl.BlockSpec((None, tt, 2 * C), lambda b_, i: (b_, i, bg)),
                  pl.BlockSpec((None, halo, 2 * C), lambda b_, i: (b_, jnp.maximum(i * th - 1, 0), bg)),
                  pl.BlockSpec((None, halo, 2 * C), lambda b_, i: (b_, jnp.minimum((i + 1) * th, nh - 1), bg)),
                  pl.BlockSpec((None, K, C), lambda b_, i: (l, 0, 0)),
                  vspec, vspec, vspec],
        out_specs=pl.BlockSpec((None, tt, C), lambda b_, i: (b_, i, 0)),
        out_shape=jax.ShapeDtypeStruct((B, T, C), BF16),
        scratch_shapes=[pltpu.VMEM((tt + 2 * halo, C), F32),
                        pltpu.VMEM((SUBLANES, tt + 2 * halo - SUBLANES, C), F32),
                        pltpu.VMEM((tt, C), F32),
                        pltpu.VMEM((K, SUBLANES, C), F32)],
        compiler_params=_params(("parallel", "parallel")),
        name="conformer_conv",
    )(z, z, z, w, vec(b), vec(lg), vec(lb))


def _rope(x, cs, sn, quarter):
    lane = lax.broadcasted_iota(jnp.int32, x.shape, 1)
    partner = jnp.where(lane % (2 * quarter) < quarter,
                        pltpu.roll(x, x.shape[1] - quarter, 1), pltpu.roll(x, quarter, 1))
    return x * cs + partner * sn


def _attn_kernel(*refs, n_src, rope, lam_init, scale, hd, quarter, rs):
    it = iter(refs)
    q_ref = next(it)
    kv = [(next(it), next(it)) for _ in range(n_src)]
    if rope:
        cq_ref, sq_ref, ck_ref, sk_ref = next(it), next(it), next(it), next(it)
    dl_ref, g_ref, o_ref, k_all, v_ext = next(it), next(it), next(it), next(it), next(it)
    i = pl.program_id(2)
    tq, W = q_ref.shape
    hp = W // hd

    @pl.when(i == 0)
    def _():
        off = 0
        for s in range(n_src):
            k_ref, v_ref = kv[s]
            n = k_ref.shape[0]
            if rope and s == 0:
                k_all[off:off + n, :] = _rope(k_ref[...].astype(F32), ck_ref[...], sk_ref[...], quarter).astype(BF16)
            else:
                k_all[off:off + n, :] = k_ref[...]
            for h in range(hp):
                v_ext[h, off:off + n, 0:hd] = v_ref[:, h * hd:(h + 1) * hd]
                v_ext[h, off:off + n, hd:2 * hd] = jnp.ones((n, hd), BF16)
            off += n

    q = q_ref[...].astype(F32)
    if rope:
        q = _rope(q, cq_ref[...], sq_ref[...], quarter)
    qb = (q * (scale * math.log2(math.e))).astype(BF16)
    keys = k_all[...]

    lq = dl_ref[...]
    lam = (jnp.exp(jnp.sum(lq[0:1] * lq[1:2], axis=-1, keepdims=True))
           - jnp.exp(jnp.sum(lq[2:3] * lq[3:4], axis=-1, keepdims=True)) + lam_init)

    lane = lax.broadcasted_iota(jnp.int32, (rs, W), 1)
    half = hd // 2

    def scores(r, ch):
        lo = ch * half
        qr = qb[r * rs:(r + 1) * rs, :]
        qc = jnp.where((lane >= lo) & (lane < lo + half), qr, jnp.zeros_like(qr))
        return lax.dot_general(qc, keys, (((1,), (1,)), ((), ())), preferred_element_type=F32)

    def weights(s):
        return jnp.exp2(s - jnp.max(s, axis=-1, keepdims=True)).astype(BF16)

    def attend(ch, e):
        ne = jnp.dot(e, v_ext[ch // 2], preferred_element_type=F32)
        return ne[:, 0:hd] / ne[:, hd:2 * hd]

    chains = [(r, ch) for r in range(tq // rs) for ch in range(2 * hp)]
    s_cur = scores(*chains[0])
    e_prev = None
    comps = []
    for n in range(len(chains)):
        s_next = scores(*chains[n + 1]) if n + 1 < len(chains) else None
        e_cur = weights(s_cur)
        if e_prev is not None:
            comps.append(attend(chains[n - 1][1], e_prev))
        e_prev, s_cur = e_cur, s_next
    comps.append(attend(chains[-1][1], e_prev))

    for n in range(0, len(chains), 2):
        r, h = chains[n][0], chains[n][1] // 2
        o = comps[n] - lam * comps[n + 1]
        y = o * lax.rsqrt(jnp.mean(o * o, axis=-1, keepdims=True) + EPS) * g_ref[...] * (1.0 - lam_init)
        o_ref[r * rs:(r + 1) * rs, h * hd:(h + 1) * hd] = y.astype(BF16)


def _attention(zq, srcs, col_q, col_ks, col_vs, tabs, da_lam, subln_g, l, lam_init, n_heads, tq):
    B, Tq, _ = zq.shape
    hd = subln_g.shape[1]
    W = min(MXU_WIDTH, n_heads * hd)
    hp = W // hd
    assert hd == LANES and n_heads % hp == 0 and col_q % W == 0 and Tq % tq == 0
    assert all(ck % W == 0 for ck in col_ks) and all(cv % W == 0 for cv in col_vs)
    bq = col_q // W
    rope = tabs is not None
    in_specs = [pl.BlockSpec((None, tq, W), lambda b, h, i: (b, i, bq + h))]
    args = [zq]
    Tk_all = 0
    for zs, ck, cv in zip(srcs, col_ks, col_vs):
        Tk = zs.shape[1]
        Tk_all += Tk
        bk, bv = ck // W, cv // W
        in_specs += [pl.BlockSpec((None, Tk, W), lambda b, h, i, bk=bk: (b, 0, bk + h)),
                     pl.BlockSpec((None, Tk, W), lambda b, h, i, bv=bv: (b, 0, bv + h))]
        args += [zs, zs]
    if rope:
        cs, sn = tabs
        in_specs += [pl.BlockSpec((tq, W), lambda b, h, i: (i, 0)), pl.BlockSpec((tq, W), lambda b, h, i: (i, 0)),
                     pl.BlockSpec((Tq, W), lambda b, h, i: (0, 0)), pl.BlockSpec((Tq, W), lambda b, h, i: (0, 0))]
        args += [cs, sn, cs, sn]
    d_qk = da_lam.shape[2]
    in_specs += [pl.BlockSpec((None,) + da_lam.shape[1:], lambda b, h, i: (l, 0, 0)),
                 pl.BlockSpec((None, 1, hd), lambda b, h, i: (l, 0, 0))]
    args += [da_lam, subln_g.reshape(subln_g.shape[0], 1, hd)]
    return pl.pallas_call(
        functools.partial(_attn_kernel, n_src=len(srcs), rope=rope, lam_init=lam_init, scale=d_qk ** -0.5,
                          hd=hd, quarter=d_qk // 4, rs=min(tq, 256)),
        grid=(B, n_heads // hp, Tq // tq),
        in_specs=in_specs,
        out_specs=pl.BlockSpec((None, tq, W), lambda b, h, i: (b, i, h)),
        out_shape=jax.ShapeDtypeStruct((B, Tq, n_heads * hd), BF16),
        scratch_shapes=[pltpu.VMEM((Tk_all, W), BF16), pltpu.VMEM((hp, Tk_all, 2 * hd), BF16)],
        compiler_params=_params(("parallel", "parallel", "arbitrary")),
        name="diff_attention",
    )(*args)


def _merge_kernel(*refs, nsplit):
    r_ref, c_ref, a_ref = refs[0:3]
    z_refs = refs[3:3 + 3 * nsplit]
    x_ref, gt_ref, w0_ref, w1_ref, w2_ref, wo_ref, o_ref = refs[3 + 3 * nsplit:]
    gw = z_refs[0].shape[1]
    ins = (r_ref[...], c_ref[...], a_ref[...])
    ws = (w0_ref, w1_ref, w2_ref)
    out = None
    for p in range(nsplit):
        cols = slice(p * gw, (p + 1) * gw)
        m = None
        for j in range(3):
            gate = _sigmoid(z_refs[j * nsplit + p][...].astype(F32))
            t = gate * jnp.dot(ins[j], ws[j][:, cols], preferred_element_type=F32)
            m = t if m is None else m + t
        t = jnp.dot(m.astype(BF16), wo_ref[cols, :], preferred_element_type=F32)
        out = t if out is None else out + t
    o_ref[...] = x_ref[...] + gt_ref[...] * out


def _merge(rnn_g, conv_g, att_g, z, col_z, s, mods, row_fn, w0, w1, w2, wo, l, tm):
    M, D = s.shape
    gw = math.gcd(col_z, D)
    nsplit = D // gw
    bz = col_z // gw
    assert M % tm == 0 and gw % LANES == 0

    def act_spec(a):
        return pl.BlockSpec((tm, a.shape[1]), lambda i: (i, 0))

    def w_spec(w):
        return pl.BlockSpec((None,) + w.shape[1:], lambda i: (l, 0, 0), pipeline_mode=pl.Buffered(1))

    z_specs = [pl.BlockSpec((tm, gw), lambda i, blk=bz + j * nsplit + p: (i, blk))
               for j in range(3) for p in range(nsplit)]
    return pl.pallas_call(
        functools.partial(_merge_kernel, nsplit=nsplit),
        grid=(M // tm,),
        in_specs=[act_spec(rnn_g), act_spec(conv_g), act_spec(att_g)] + z_specs + [
            pl.BlockSpec((tm, D), lambda i: (i, 0)),
            _mod_spec(row_fn, 5, D),
            w_spec(w0), w_spec(w1), w_spec(w2), w_spec(wo)],
        out_specs=pl.BlockSpec((tm, D), lambda i: (i, 0)),
        out_shape=jax.ShapeDtypeStruct((M, D), F32),
        compiler_params=_params(("parallel",)),
        name="merge_out_proj",
    )(rnn_g, conv_g, att_g, *([z] * (3 * nsplit)), s, mods, w0, w1, w2, wo)


def _rope_tables(T, d_qk, width):
    n = d_qk // 4
    rows = T // GRID_W
    row = jnp.repeat(jnp.arange(rows, dtype=jnp.int32), GRID_W).astype(F32)
    col = jnp.tile(jnp.arange(GRID_W, dtype=jnp.int32), rows).astype(F32)
    inv = ROPE_BASE ** (-jnp.arange(n, dtype=F32) * 2.0 / (2 * n))
    ar, ac = row[:, None] * inv, col[:, None] * inv
    cs = jnp.concatenate([jnp.cos(ar), jnp.cos(ar), jnp.cos(ac), jnp.cos(ac)], axis=1)
    sn = jnp.concatenate([-jnp.sin(ar), jnp.sin(ar), -jnp.sin(ac), jnp.sin(ac)], axis=1)
    return jnp.tile(cs, (1, width // d_qk)), jnp.tile(sn, (1, width // d_qk))


def _block_diag(w, tc):
    depth, two, nb, bw, _ = w.shape
    per = tc // bw
    w6 = w.reshape(depth, two, nb // per, per, bw, bw)
    eye = jnp.eye(per, dtype=w.dtype)
    bd = jnp.einsum('ldtpij,pq->ldtpiqj', w6, eye)
    return bd.reshape(depth, two, nb // per, tc, tc).astype(BF16)


def kernel(x, c, ctx, c_ctx, ada_w, ada_b, norm_g, ffn_w_gate, ffn_w_up, ffn_w_down, w_in, rnn_conv_w, rnn_conv_b, rg_w_r, rg_b_r, rg_w_i, rg_b_i, rg_lam, rnn_w_out, cv_dw_w, cv_dw_b, cv_ln_g, cv_ln_b, cv_w_out, da_lam, da_subln_g, da_w_o, w_out, final_g):
    B, T, D = x.shape
    Tc = ctx.shape[1]
    depth = ada_w.shape[0]
    d_rnn = rnn_conv_w.shape[2]
    d_conv = cv_dw_w.shape[2]
    d_qk = da_lam.shape[2]
    d_v = da_subln_g.shape[1]
    d_att = da_w_o.shape[1]
    n_heads = d_att // d_v
    d_q = n_heads * 2 * d_qk
    splits = (d_rnn, d_rnn, 2 * d_conv, d_q, d_q, d_att, 3 * D)
    assert sum(splits) == w_in.shape[2] and 2 * d_qk == d_v
    offs = [0]
    for w_ in splits[:-1]:
        offs.append(offs[-1] + w_)
    col_x, col_y, col_g, col_q, col_k, col_v, col_z = offs

    tl = _tiles(T, T)
    tcx = _tiles(Tc, B * Tc)
    tc_rnn = min(tl["tc_rnn"], d_rnn)
    att_w = min(MXU_WIDTH, n_heads * d_v)

    R = -(-(B + 1) // SUBLANES) * SUBLANES
    cc = jnp.zeros((R, D), F32).at[:B].set(c).at[B].set(c_ctx)
    mods_all = _ada(cc, ada_w, ada_b).reshape(depth, R * N_MOD, 1, D)

    tabs = _rope_tables(T, d_qk, att_w)

    tn_in = math.gcd(math.gcd(d_rnn, col_k), math.gcd(d_q, d_att))
    ctx_blocks = tuple(range(col_x // tn_in, (col_x + d_rnn) // tn_in)) + \
        tuple(range(col_k // tn_in, (col_v + d_att) // tn_in))

    wg, wu, wd = ffn_w_gate.astype(BF16), ffn_w_up.astype(BF16), ffn_w_down.astype(BF16)
    wr_bd = _block_diag(rg_w_r, tc_rnn)
    wi_bd = _block_diag(rg_w_i, tc_rnn)
    w_rnn, w_cv, w_da, w_o = (rnn_w_out.astype(BF16), cv_w_out.astype(BF16), da_w_o.astype(BF16),
                              w_out.astype(BF16))

    def lat_row(tm):
        per = T // tm
        return lambda i: i // per

    def ctx_row(tm):
        return lambda i: B

    xl = x.reshape(B * T, D)
    xc = ctx.reshape(B * Tc, D)
    for l in range(depth):
        need_ctx = l < depth - 1
        last = l == depth - 1
        lam_init = 0.8 - 0.6 * math.exp(-0.3 * l)
        mods = mods_all[l]

        xl = _ffn(xl, mods, lat_row(tl["tm_ffn"]), 0, norm_g[l, 0], wg, wu, wd, l, 0, tl["tm_ffn"], tl["tf"])
        xc = _ffn(xc, mods, ctx_row(tcx["tm_ffn"]), 0, norm_g[l, 0], wg, wu, wd, l, 0, tcx["tm_ffn"], tcx["tf"])

        zl = _in_proj(xl, mods, lat_row(tl["tm_in"]), norm_g[l, 1], w_in, l,tl["tm_in"], tn_in)
        if need_ctx:
            zc = _in_proj(xc, mods, ctx_row(tcx["tm_in"]), norm_g[l, 1], w_in, l,tcx["tm_in"], tn_in)
            cxc, cyc, ckc, cvc = col_x, col_y, col_k, col_v
        else:
            zc = _in_proj(xc, mods, ctx_row(tcx["tm_in"]), norm_g[l, 1], w_in, l,tcx["tm_in"], tn_in,
                          col_blocks=ctx_blocks)
            cxc, cyc, ckc, cvc = 0, None, d_rnn, d_rnn + d_q
        zl3 = zl.reshape(B, T, -1)
        zc3 = zc.reshape(B, Tc, -1)

        rnn_l, rnn_c = _rnn(zl3, zc3, col_x, col_y, cxc, cyc, rnn_conv_w, rnn_conv_b, wr_bd, rg_b_r, wi_bd,
                            rg_b_i, rg_lam, l, need_ctx, tc_rnn)
        conv_l = _conformer(zl3, col_g, cv_dw_w, cv_dw_b, cv_ln_g, cv_ln_b, l, tl["tt_conv"])
        att_l = _attention(zl3, [zl3, zc3], col_q, [col_k, ckc], [col_v, cvc], tabs, da_lam, da_subln_g, l,
                           lam_init, n_heads, tl["tq"])
        xl = _merge(rnn_l.reshape(B * T, -1), conv_l.reshape(B * T, -1), att_l.reshape(B * T, -1), zl, col_z,
                    xl, mods, lat_row(tl["tm_merge"]), w_rnn, w_cv, w_da, w_o, l, tl["tm_merge"])
        xl = _ffn(xl, mods, lat_row(tl["tm_ffn"]), 2, norm_g[l, 2], wg, wu, wd, l, 1, tl["tm_ffn"], tl["tf"],
                  final_g=final_g if last else None)
        if need_ctx:
            conv_c = _conformer(zc3, col_g, cv_dw_w, cv_dw_b, cv_ln_g, cv_ln_b, l, tcx["tt_conv"])
            att_c = _attention(zc3, [zc3], col_q, [col_k], [col_v], None, da_lam, da_subln_g, l,
                               lam_init, n_heads, tcx["tq"])
            xc = _merge(rnn_c.reshape(B * Tc, -1), conv_c.reshape(B * Tc, -1), att_c.reshape(B * Tc, -1), zc, col_z,
                        xc, mods, ctx_row(tcx["tm_merge"]), w_rnn, w_cv, w_da, w_o, l, tcx["tm_merge"])
            xc = _ffn(xc, mods, ctx_row(tcx["tm_ffn"]), 2, norm_g[l, 2], wg, wu, wd, l, 1,
                      tcx["tm_ffn"], tcx["tf"])
    return xl.reshape(B, T, D)
```
